```python
import math
import jax
import jax.numpy as jnp
from jax import lax
import numpy as np

D_MODEL = 2048
BATCH = 16
SEQ = 256
DEPTH = 4
DEC_BATCH = 8
DEC_SEQ = 1024
PAST_LEN = 256

GRID_W = 64
HEAD_DIM = 128
H_A = 8
KV_A = 2
WINDOW = 128
H_B = 8
KV_B = 2
H_C = 8
DK_C = 128
DV_C = 128
RET_CHUNK = 128
H_D = 4
DH_D = 128
Q_BLOCK = 128
ROPE_THETA = 10000.0
N_EXPERTS = 32
TOP_K = 4
D_FF = 2048
SWIGLU_LIMIT = 7.0
SWIGLU_ALPHA = 1.702
MOE_BLOCK = 128
N_EVEN = (DEPTH + 1) // 2
N_ODD = DEPTH // 2
DEEPNORM_ALPHA = (2 * DEPTH) ** 0.25
DEEPNORM_BETA = (8 * DEPTH) ** -0.25
NORM_EPS = 1e-6
NEG_INF = -1e30
EVEN_SPLITS = (H_A * HEAD_DIM, KV_A * HEAD_DIM, KV_A * HEAD_DIM, H_B * HEAD_DIM, KV_B * HEAD_DIM, KV_B * HEAD_DIM)
ODD_SPLITS = (H_C * DK_C, H_C * DK_C, H_C * DV_C, H_C * DV_C, H_C * DV_C, H_D * 2 * DH_D, H_D * 2 * DH_D, H_D * 2 * DH_D)
EVEN_IN = sum(EVEN_SPLITS)
ODD_IN = sum(ODD_SPLITS)
EVEN_MIX = (H_A + H_B) * HEAD_DIM
ODD_MIX = H_C * DV_C + H_D * 2 * DH_D

kernel_name = 'hybrid_diffusion_prefix_step'


def _split(x, sizes):
    idx = [int(i) for i in np.cumsum(sizes)[:-1]]
    return jnp.split(x, idx, axis=-1)


def _layer_norm(x, g, b):
    xf = x.astype(jnp.float32)
    mu = jnp.mean(xf, -1, keepdims=True)
    var = jnp.mean(jnp.square(xf - mu), -1, keepdims=True)
    y = (xf - mu) * lax.rsqrt(var + NORM_EPS) * g.astype(jnp.float32) + b.astype(jnp.float32)
    return y.astype(x.dtype)


def _rms_norm(x, g):
    xf = x.astype(jnp.float32)
    y = xf * lax.rsqrt(jnp.mean(jnp.square(xf), -1, keepdims=True) + NORM_EPS) * g.astype(jnp.float32)
    return y.astype(x.dtype)


def _head_norm(x):
    mu = jnp.mean(x, -1, keepdims=True)
    var = jnp.mean(jnp.square(x - mu), -1, keepdims=True)
    return (x - mu) * lax.rsqrt(var + NORM_EPS)


def _grid_rope_tables(n_tokens, dim):
    n_rows = n_tokens // GRID_W
    t = jnp.arange(n_rows * GRID_W)
    rows = (t // GRID_W).astype(jnp.float32)
    cols = (t % GRID_W).astype(jnp.float32)
    n_freq = dim // 4
    inv = ROPE_THETA ** (-jnp.arange(n_freq, dtype=jnp.float32) / n_freq)
    ar = (rows[:, None] * inv)[:, None, :]
    ac = (cols[:, None] * inv)[:, None, :]
    return (jnp.cos(ar), jnp.sin(ar), jnp.cos(ac), jnp.sin(ac))


def _rope(x, tabs):
    cr, sr, cc, sc = tabs
    xf = x.astype(jnp.float32)
    half = x.shape[-1] // 2

    def rot(u, cos, sin):
        u1, u2 = jnp.split(u, 2, axis=-1)
        return jnp.concatenate([u1 * cos - u2 * sin, u2 * cos + u1 * sin], -1)

    return jnp.concatenate([rot(xf[..., :half], cr, sr), rot(xf[..., half:], cc, sc)], -1).astype(x.dtype)


def _to_heads(x, n_kv):
    b, t, h, d = x.shape
    return x.reshape(b, t, n_kv, h // n_kv, d).transpose(0, 2, 3, 1, 4)


def _from_heads(o):
    b, k, g, t, d = o.shape
    return o.transpose(0, 3, 1, 2, 4).reshape(b, t, k * g * d)


def _dense_attn(q, k, v, scale, sink=None):
    b, kv, g, tq, d = q.shape
    nb = tq // Q_BLOCK
    qb = jnp.moveaxis(q.reshape(b, kv, g, nb, Q_BLOCK, d), 3, 0)

    def block(qblk):
        s = jnp.einsum('bkgqd,bskd->bkgqs', qblk, k).astype(jnp.float32) * scale
        if sink is None:
            p = jax.nn.softmax(s, axis=-1)
        else:
            sk = jnp.broadcast_to(sink.astype(jnp.float32)[None, :, :, None, None], s.shape[:-1] + (1,))
            p = jax.nn.softmax(jnp.concatenate([s, sk], -1), axis=-1)[..., :-1]
        return jnp.einsum('bkgqs,bskd->bkgqd', p.astype(v.dtype), v)

    o = lax.map(block, qb)
    return jnp.moveaxis(o, 0, 3).reshape(b, kv, g, tq, v.shape[-1])


def _window_attn(q, k, v, ck, cv, sink, scale):
    b, kv, g, t, d = q.shape
    nb = t // WINDOW
    qb = q.reshape(b, kv, g, nb, WINDOW, d)

    def bands(a):
        ap = jnp.pad(a, ((0, 0), (WINDOW, WINDOW), (0, 0), (0, 0))).reshape(b, nb + 2, WINDOW, kv, a.shape[-1])
        return jnp.concatenate([ap[:, :-2], ap[:, 1:-1], ap[:, 2:]], axis=2)

    kw, vw = bands(k), bands(v)
    qi = jnp.arange(WINDOW)[:, None]
    kj = jnp.arange(3 * WINDOW)[None, :]
    kpos = jnp.arange(nb)[:, None, None] * WINDOW - WINDOW + kj[None]
    rel = kj - qi
    mask = (rel >= 0) & (rel <= 2 * WINDOW) & (kpos >= 0) & (kpos < t)
    s_loc = jnp.einsum('bkgnqd,bnskd->bkgnqs', qb, kw).astype(jnp.float32) * scale
    s_loc = jnp.where(mask, s_loc, NEG_INF)
    s_ctx = jnp.einsum('bkgnqd,blkd->bkgnql', qb, ck).astype(jnp.float32) * scale
    s_sink = jnp.broadcast_to(sink.astype(jnp.float32)[None, :, :, None, None, None], s_loc.shape[:-1] + (1,))
    p = jax.nn.softmax(jnp.concatenate([s_loc, s_ctx, s_sink], -1), axis=-1)
    n_loc = 3 * WINDOW
    n_ctx = ck.shape[1]
    o = (jnp.einsum('bkgnqs,bnskd->bkgnqd', p[..., :n_loc].astype(v.dtype), vw)
         + jnp.einsum('bkgnql,blkd->bkgnqd', p[..., n_loc:n_loc + n_ctx].astype(cv.dtype), cv))
    return o.reshape(b, kv, g, t, v.shape[-1])


def _retention(q, k, v, log_g, s0):
    b, h, t, dk = q.shape
    dv = v.shape[-1]
    n = t // RET_CHUNK
    pos = jnp.arange(RET_CHUNK, dtype=jnp.float32)
    diff = pos[:, None] - pos[None, :]
    dmat = jnp.where(diff >= 0, jnp.exp(jnp.maximum(diff, 0.0)[None] * log_g[:, None, None]), 0.0)
    xi = jnp.exp((pos + 1.0)[None] * log_g[:, None])
    zeta = jnp.exp((RET_CHUNK - 1.0 - pos)[None] * log_g[:, None])
    g_chunk = jnp.exp(RET_CHUNK * log_g)[:, None, None]

    def chunks(a):
        return jnp.moveaxis(a.reshape(b, h, n, RET_CHUNK, a.shape[-1]), 2, 0)

    def step(s, qkv):
        qc, kc, vc = qkv
        att = jnp.einsum('bhid,bhjd->bhij', qc, kc) * dmat
        o = jnp.einsum('bhij,bhje->bhie', att, vc) + jnp.einsum('bhid,bhde->bhie', qc * xi[..., None], s)
        s = g_chunk * s + jnp.einsum('bhjd,bhje->bhde', kc * zeta[..., None], vc)
        return s, o

    s_fin, o = lax.scan(step, s0, (chunks(q), chunks(k), chunks(v)))
    return jnp.moveaxis(o, 0, 2).reshape(b, h, t, dv), s_fin


def _even_mixer(h, w_in, w_out, sink, q_gain, k_gain, ctx):
    b, t, _ = h.shape
    qa, ka, va, qb, kb, vb = _split(h @ w_in, EVEN_SPLITS)
    qa = qa.reshape(b, t, H_A, HEAD_DIM)
    ka = ka.reshape(b, t, KV_A, HEAD_DIM)
    va = va.reshape(b, t, KV_A, HEAD_DIM)
    qb = _rms_norm(qb.reshape(b, t, H_B, HEAD_DIM), q_gain)
    kb = _rms_norm(kb.reshape(b, t, KV_B, HEAD_DIM), k_gain)
    vb = vb.reshape(b, t, KV_B, HEAD_DIM)
    sink = sink.reshape(KV_A, H_A // KV_A)
    scale = HEAD_DIM ** -0.5
    if ctx is None:
        oa = _dense_attn(_to_heads(qa, KV_A), ka, va, scale, sink)
        ob = _dense_attn(_to_heads(qb, KV_B), kb, vb, scale)
    else:
        cka, cva, ckb, cvb = ctx
        tabs = _grid_rope_tables(t, HEAD_DIM)
        oa = _window_attn(_to_heads(_rope(qa, tabs), KV_A), _rope(ka, tabs), va,
                          cka.astype(ka.dtype), cva.astype(va.dtype), sink, scale)
        ob = _dense_attn(_to_heads(_rope(qb, tabs), KV_B),
                         jnp.concatenate([_rope(kb, tabs), ckb.astype(kb.dtype)], 1),
                         jnp.concatenate([vb, cvb.astype(vb.dtype)], 1), scale)
    out = jnp.concatenate([_from_heads(oa).astype(h.dtype), _from_heads(ob).astype(h.dtype)], -1) @ w_out
    return out, (ka, va, kb, vb)


def _odd_mixer(h, w_in, w_out, ret_decay, lam, subln, layer, ctx):
    b, t, _ = h.shape
    f32 = jnp.float32
    qc, kc, vc, gf, gb, qd, kd, vd = _split(h @ w_in, ODD_SPLITS)
    qc = qc.reshape(b, t, H_C, DK_C)
    kc = kc.reshape(b, t, H_C, DK_C) * (DK_C ** -0.5)
    vc = vc.reshape(b, t, H_C, DV_C)
    qd = qd.reshape(b, t, H_D, 2 * DH_D)
    kd = kd.reshape(b, t, H_D, 2 * DH_D)
    vd = vd.reshape(b, t, H_D, 2 * DH_D)
    if ctx is None:
        s0f = jnp.zeros((b, H_C, DK_C, DV_C), f32)
        s0b = jnp.zeros((b, H_C, DK_C, DV_C), f32)
        keys, vals = kd, vd
    else:
        s0f, s0b, ckd, cvd = ctx
        tabs_c = _grid_rope_tables(t, DK_C)
        qc = _rope(qc, tabs_c)
        kc = _rope(kc, tabs_c)
        tabs_d = _grid_rope_tables(t, DH_D)
        qd = _rope(qd.reshape(b, t, 2 * H_D, DH_D), tabs_d).reshape(b, t, H_D, 2 * DH_D)
        kdr = _rope(kd.reshape(b, t, 2 * H_D, DH_D), tabs_d).reshape(b, t, H_D, 2 * DH_D)
        keys = jnp.concatenate([kdr, ckd.astype(kd.dtype)], 1)
        vals = jnp.concatenate([vd, cvd.astype(vd.dtype)], 1)
    log_g = jax.nn.log_sigmoid(ret_decay.astype(f32))
    qh = jnp.swapaxes(qc, 1, 2).astype(f32)
    kh = jnp.swapaxes(kc, 1, 2).astype(f32)
    vh = jnp.swapaxes(vc, 1, 2).astype(f32)
    of, sf = _retention(qh, kh, vh, log_g[0], s0f.astype(f32))
    ob, sb = _retention(qh[:, :, ::-1], kh[:, :, ::-1], vh[:, :, ::-1], log_g[1], s0b.astype(f32))
    ob = ob[:, :, ::-1]
    yc = (jax.nn.silu(gf.reshape(b, t, H_C, DV_C).astype(f32)) * _head_norm(jnp.swapaxes(of, 1, 2))
          + jax.nn.silu(gb.reshape(b, t, H_C, DV_C).astype(f32)) * _head_norm(jnp.swapaxes(ob, 1, 2)))
    yc = yc.reshape(b, t, H_C * DV_C).astype(h.dtype)
    lam_init = 0.8 - 0.6 * math.exp(-0.3 * layer)
    lam32 = lam.astype(f32)
    lam_val = jnp.exp(jnp.sum(lam32[0] * lam32[1])) - jnp.exp(jnp.sum(lam32[2] * lam32[3])) + lam_init
    qdh = _to_heads(qd, H_D)
    scale = DH_D ** -0.5
    a1 = _dense_attn(qdh[..., :DH_D], keys[..., :DH_D], vals, scale)
    a2 = _dense_attn(qdh[..., DH_D:], keys[..., DH_D:], vals, scale)
    od = _rms_norm(a1.astype(f32) - lam_val * a2.astype(f32), subln) * (1.0 - lam_init)
    yd = _from_heads(od).astype(h.dtype)
    out = jnp.concatenate([yc, yd], -1) @ w_out
    return out, (sf, sb, kd, vd)


def _moe(x, w_router, b_router, w_gate, b_gate, w_up, b_up, w_down, b_down):
    n_tok, d = x.shape
    n_assign = n_tok * TOP_K
    logits = (x @ w_router + b_router).astype(jnp.float32)
    top_v, top_e = lax.top_k(logits, TOP_K)
    gates = jax.nn.softmax(top_v, axis=-1).reshape(n_assign)
    flat_e = top_e.reshape(n_assign).astype(jnp.int32)
    order = jnp.argsort(flat_e).astype(jnp.int32)
    sorted_e = flat_e[order]
    counts = jnp.zeros((N_EXPERTS,), jnp.int32).at[flat_e].add(1)
    padded = (counts + MOE_BLOCK - 1) // MOE_BLOCK * MOE_BLOCK
    start = jnp.cumsum(counts) - counts
    ends_p = jnp.cumsum(padded)
    start_p = ends_p - padded
    dest = start_p[sorted_e] + jnp.arange(n_assign, dtype=jnp.int32) - start[sorted_e]
    n_blocks = (n_assign + N_EXPERTS * (MOE_BLOCK - 1) + MOE_BLOCK - 1) // MOE_BLOCK
    n_rows = n_blocks * MOE_BLOCK
    row_src = jnp.full((n_rows,), n_assign, jnp.int32).at[dest].set(order)
    row_tok = row_src // TOP_K
    row_w = jnp.concatenate([gates, jnp.zeros((1,), jnp.float32)])[row_src]
    block_e = jnp.minimum(jnp.searchsorted(ends_p, jnp.arange(n_blocks, dtype=jnp.int32) * MOE_BLOCK, side='right'),
                          N_EXPERTS - 1)
    xb = jnp.concatenate([x, jnp.zeros((1, d), x.dtype)], 0)[row_tok].reshape(n_blocks, MOE_BLOCK, d)

    def expert_block(args):
        xblk, e = args
        g = jnp.minimum(xblk @ w_gate[e] + b_gate[e], SWIGLU_LIMIT)
        u = jnp.clip(xblk @ w_up[e] + b_up[e], -SWIGLU_LIMIT, SWIGLU_LIMIT)
        return (g * jax.nn.sigmoid(SWIGLU_ALPHA * g) * (u + 1.0)) @ w_down[e] + b_down[e]

    yb = lax.map(expert_block, (xb, block_e)).reshape(n_rows, d)
    y = jax.ops.segment_sum(yb.astype(jnp.float32) * row_w[:, None], row_tok, num_segments=n_tok + 1)[:n_tok]
    return y.astype(x.dtype)


def setup_inputs(seed: int = 0) -> dict:
    key = jax.random.key(seed)
    ks = iter(jax.random.split(key, 40))

    def nrm(shape, scale):
        return jax.random.normal(next(ks), shape, jnp.float32) * scale

    decay_init = jnp.log(2.0 ** (5.0 + jnp.arange(H_C, dtype=jnp.float32)) - 1.0)
    return {
        'x_prompt': nrm((BATCH, SEQ, D_MODEL), 1.0),
        'x_sample': nrm((DEC_BATCH, DEC_SEQ, D_MODEL), 1.0),
        'c': nrm((DEC_BATCH, D_MODEL), 1.0),
        'cache_k_a': nrm((DEC_BATCH, N_EVEN, PAST_LEN, KV_A, HEAD_DIM), 1.0),
        'cache_v_a': nrm((DEC_BATCH, N_EVEN, PAST_LEN, KV_A, HEAD_DIM), 1.0),
        'cache_k_b': nrm((DEC_BATCH, N_EVEN, PAST_LEN, KV_B, HEAD_DIM), 1.0),
        'cache_v_b': nrm((DEC_BATCH, N_EVEN, PAST_LEN, KV_B, HEAD_DIM), 1.0),
        'state_ret_fwd': nrm((DEC_BATCH, N_ODD, H_C, DK_C, DV_C), 1.0),
        'state_ret_bwd': nrm((DEC_BATCH, N_ODD, H_C, DK_C, DV_C), 1.0),
        'cache_k_d': nrm((DEC_BATCH, N_ODD, PAST_LEN, H_D, 2 * DH_D), 1.0),
        'cache_v_d': nrm((DEC_BATCH, N_ODD, PAST_LEN, H_D, 2 * DH_D), 1.0),
        'c_ctx': nrm((D_MODEL,), 1.0),
        'w_ada': nrm((DEPTH, D_MODEL, 6 * D_MODEL), 0.5 * D_MODEL ** -0.5),
        'b_ada': nrm((DEPTH, 6 * D_MODEL), 0.02),
        'ln_g': 1.0 + nrm((DEPTH, 2, D_MODEL), 0.02),
        'ln_b': nrm((DEPTH, 2, D_MODEL), 0.02),
        'w_in_ab': nrm((N_EVEN, D_MODEL, EVEN_IN), D_MODEL ** -0.5),
        'w_out_ab': nrm((N_EVEN, EVEN_MIX, D_MODEL), DEEPNORM_BETA * EVEN_MIX ** -0.5),
        'sink_a': nrm((N_EVEN, H_A), 0.5),
        'qnorm_b': 1.0 + nrm((N_EVEN, HEAD_DIM), 0.02),
        'knorm_b': 1.0 + nrm((N_EVEN, HEAD_DIM), 0.02),
        'w_in_cd': nrm((N_ODD, D_MODEL, ODD_IN), D_MODEL ** -0.5),
        'w_out_cd': nrm((N_ODD, ODD_MIX, D_MODEL), DEEPNORM_BETA * ODD_MIX ** -0.5),
        'ret_decay': decay_init[None, None, :] + nrm((N_ODD, 2, H_C), 0.01),
        'lam_d': nrm((N_ODD, 4, DH_D), 0.1),
        'subln_d': 1.0 + nrm((N_ODD, 2 * DH_D), 0.02),
        'w_router': nrm((DEPTH, D_MODEL, N_EXPERTS), D_MODEL ** -0.5),
        'b_router': nrm((DEPTH, N_EXPERTS), 0.01),
        'w_gate': nrm((DEPTH, N_EXPERTS, D_MODEL, D_FF), D_MODEL ** -0.5),
        'b_gate': nrm((DEPTH, N_EXPERTS, D_FF), 0.02),
        'w_up': nrm((DEPTH, N_EXPERTS, D_MODEL, D_FF), D_MODEL ** -0.5),
        'b_up': nrm((DEPTH, N_EXPERTS, D_FF), 0.02),
        'w_down': nrm((DEPTH, N_EXPERTS, D_FF, D_MODEL), DEEPNORM_BETA * D_FF ** -0.5),
        'b_down': nrm((DEPTH, N_EXPERTS, D_MODEL), 0.02),
    }


def reference(x_prompt, x_sample, c, cache_k_a, cache_v_a, cache_k_b, cache_v_b, state_ret_fwd, state_ret_bwd,
              cache_k_d, cache_v_d, c_ctx, w_ada, b_ada, ln_g, ln_b, w_in_ab, w_out_ab, sink_a, qnorm_b, knorm_b,
              w_in_cd, w_out_cd, ret_decay, lam_d, subln_d, w_router, b_router, w_gate, b_gate, w_up, b_up,
              w_down, b_down):
    xp, xs = x_prompt, x_sample
    n_p = xp.shape[0] * xp.shape[1]
    new_k_a, new_v_a, new_k_b, new_v_b = [], [], [], []
    new_s_f, new_s_b, new_k_d, new_v_d = [], [], [], []
    for l in range(DEPTH):
        i = l // 2
        mod_p = _split(jax.nn.silu(c_ctx) @ w_ada[l] + b_ada[l], (D_MODEL,) * 6)
        mod_s = [m[:, None, :] for m in _split(jax.nn.silu(c) @ w_ada[l] + b_ada[l], (D_MODEL,) * 6)]
        hp = xp * (1.0 + mod_p[1]) + mod_p[0]
        hs = xs * (1.0 + mod_s[1]) + mod_s[0]
        if l % 2 == 0:
            op, (ka, va, kb, vb) = _even_mixer(hp, w_in_ab[i], w_out_ab[i], sink_a[i], qnorm_b[i], knorm_b[i], None)
            os_, _ = _even_mixer(hs, w_in_ab[i], w_out_ab[i], sink_a[i], qnorm_b[i], knorm_b[i],
                                 (cache_k_a[:, i], cache_v_a[:, i], cache_k_b[:, i], cache_v_b[:, i]))
            new_k_a.append(ka)
            new_v_a.append(va)
            new_k_b.append(kb)
            new_v_b.append(vb)
        else:
            op, (sf, sb, kd, vd) = _odd_mixer(hp, w_in_cd[i], w_out_cd[i], ret_decay[i], lam_d[i], subln_d[i], l, None)
            os_, _ = _odd_mixer(hs, w_in_cd[i], w_out_cd[i], ret_decay[i], lam_d[i], subln_d[i], l,
                                (state_ret_fwd[:, i], state_ret_bwd[:, i], cache_k_d[:, i], cache_v_d[:, i]))
            new_s_f.append(sf)
            new_s_b.append(sb)
            new_k_d.append(kd)
            new_v_d.append(vd)
        xp = _layer_norm(DEEPNORM_ALPHA * xp + mod_p[2] * op, ln_g[l, 0], ln_b[l, 0])
        xs = _layer_norm(DEEPNORM_ALPHA * xs + mod_s[2] * os_, ln_g[l, 0], ln_b[l, 0])
        hp = xp * (1.0 + mod_p[4]) + mod_p[3]
        hs = xs * (1.0 + mod_s[4]) + mod_s[3]
        f = _moe(jnp.concatenate([hp.reshape(n_p, D_MODEL), hs.reshape(-1, D_MODEL)], 0),
                 w_router[l], b_router[l], w_gate[l], b_gate[l], w_up[l], b_up[l], w_down[l], b_down[l])
        fp = f[:n_p].reshape(xp.shape)
        fs = f[n_p:].reshape(xs.shape)
        xp = _layer_norm(DEEPNORM_ALPHA * xp + mod_p[5] * fp, ln_g[l, 1], ln_b[l, 1])
        xs = _layer_norm(DEEPNORM_ALPHA * xs + mod_s[5] * fs, ln_g[l, 1], ln_b[l, 1])
    return (xp, xs, jnp.stack(new_k_a, 1), jnp.stack(new_v_a, 1), jnp.stack(new_k_b, 1), jnp.stack(new_v_b, 1),
            jnp.stack(new_s_f, 1), jnp.stack(new_s_b, 1), jnp.stack(new_k_d, 1), jnp.stack(new_v_d, 1))
```

```python
import functools
import math

import jax
import jax.numpy as jnp
from jax import lax
from jax.experimental import pallas as pl
from jax.experimental.pallas import tpu as pltpu

F32 = jnp.float32
BF16 = jnp.bfloat16

D_MODEL = 2048
BATCH = 16
SEQ = 256
DEPTH = 4
DEC_BATCH = 8
DEC_SEQ = 1024
PAST_LEN = 256
GRID_W = 64
HEAD_DIM = 128
H_A = 8
KV_A = 2
WINDOW = 128
H_B = 8
KV_B = 2
H_C = 8
H_D = 4
RET_CHUNK = 128
ROPE_THETA = 10000.0
N_EXPERTS = 32
TOP_K = 4
D_FF = 2048
SWIGLU_LIMIT = 7.0
SWIGLU_ALPHA = 1.702
DEEPNORM_ALPHA = (2 * DEPTH) ** 0.25
NORM_EPS = 1e-6

N_PROMPT = BATCH * SEQ
N_SAMPLE = DEC_BATCH * DEC_SEQ
N_TOK = N_PROMPT + N_SAMPLE
SEG = DEC_SEQ
N_SEG = N_TOK // SEG
EVEN_IN = 3072
ODD_IN = 8192

MOE_TM = 256
MOE_TN = 512
N_ASSIGN = N_TOK * TOP_K
MOE_BLOCKS = (N_ASSIGN + N_EXPERTS * (MOE_TM - 1) + MOE_TM - 1) // MOE_TM
MOE_ROWS = MOE_BLOCKS * MOE_TM

MIB = 1024 * 1024


def _params(vmem_mib, n_axes):
    return pltpu.CompilerParams(
        dimension_semantics=("arbitrary",) * n_axes,
        vmem_limit_bytes=vmem_mib * MIB)


def _dot(a, b):
    return jnp.dot(a, b, preferred_element_type=F32)


def _dot_nt(a, b):
    return lax.dot_general(a, b, (((1,), (1,)), ((), ())), preferred_element_type=F32)


def _layer_norm(z, g, b):
    mu = jnp.mean(z, -1, keepdims=True)
    zc = z - mu
    var = jnp.mean(zc * zc, -1, keepdims=True)
    return zc * lax.rsqrt(var + NORM_EPS) * g + b


def _ada_kernel(c_ref, w_ref, b_ref, o_ref):
    c = c_ref[...]
    a = c * jax.nn.sigmoid(c)
    o_ref[0] = _dot(a.astype(BF16), w_ref[0].astype(BF16)) + b_ref[0]


def _ada(cond, w_ada, b_ada):
    tn = 1024
    n = w_ada.shape[-1]
    return pl.pallas_call(
        _ada_kernel,
        grid=(DEPTH, n // tn),
        in_specs=[
            pl.BlockSpec((16, D_MODEL), lambda l, j: (0, 0)),
            pl.BlockSpec((1, D_MODEL, tn), lambda l, j: (l, 0, j)),
            pl.BlockSpec((1, 1, tn), lambda l, j: (l, 0, j)),
        ],
        out_specs=pl.BlockSpec((1, 16, tn), lambda l, j: (l, 0, j)),
        out_shape=jax.ShapeDtypeStruct((DEPTH, 16, n), F32),
        compiler_params=_params(40, 2),
        name="ada",
    )(cond, w_ada, b_ada.reshape(DEPTH, 1, n))


def _proj_kernel(x_ref, sc_ref, sh_ref, w_ref, o_ref, wb_ref):
    @pl.when(pl.program_id(1) == 0)
    def _():
        wb_ref[...] = w_ref[...].astype(BF16)

    h = x_ref[...] * (1.0 + sc_ref[0]) + sh_ref[0]
    o_ref[...] = _dot(h.astype(BF16), wb_ref[...])


def _proj(x, scale, shift, w):
    tm, tn = 512, 1024
    m, k = x.shape
    n = w.shape[1]
    return pl.pallas_call(
        _proj_kernel,
        grid=(n // tn, m // tm),
        in_specs=[
            pl.BlockSpec((tm, k), lambda j, i: (i, 0)),
            pl.BlockSpec((1, 1, k), lambda j, i: (i * tm // SEG, 0, 0)),
            pl.BlockSpec((1, 1, k), lambda j, i: (i * tm // SEG, 0, 0)),
            pl.BlockSpec((k, tn), lambda j, i: (0, j)),
        ],
        out_specs=pl.BlockSpec((tm, tn), lambda j, i: (i, j)),
        out_shape=jax.ShapeDtypeStruct((m, n), F32),
        scratch_shapes=[pltpu.VMEM((k, tn), BF16)],
        compiler_params=_params(48, 2),
        name="proj",
    )(x, scale, shift, w)


def _oproj_kernel(a1_ref, a2_ref, w1_ref, w2_ref, x_ref, gate_ref, sc_ref, sh_ref,
                  lng_ref, lnb_ref, wr_ref, br_ref, xo_ref, h_ref, lg_ref):
    y = _dot(a1_ref[...], w1_ref[...]) + _dot(a2_ref[...], w2_ref[...])
    z = DEEPNORM_ALPHA * x_ref[...] + gate_ref[0] * y
    xn = _layer_norm(z, lng_ref[...], lnb_ref[...])
    xo_ref[...] = xn
    h = xn * (1.0 + sc_ref[0]) + sh_ref[0]
    h_ref[...] = h
    lg_ref[...] = _dot(h.astype(BF16), wr_ref[...]) + br_ref[...]


def _oproj(a1, a2, w_out, x, gate, scale, shift, ln_g, ln_b, w_router, b_router):
    tm = 256
    m, d = x.shape
    half = a1.shape[1]
    wb = w_out.astype(BF16)
    seg = lambda i: (i * tm // SEG, 0, 0)
    row = lambda i: (i, 0)
    fix = lambda i: (0, 0)
    return pl.pallas_call(
        _oproj_kernel,
        grid=(m // tm,),
        in_specs=[
            pl.BlockSpec((tm, half), row),
            pl.BlockSpec((tm, half), row),
            pl.BlockSpec((half, d), fix),
            pl.BlockSpec((half, d), lambda i: (1, 0)),
            pl.BlockSpec((tm, d), row),
            pl.BlockSpec((1, 1, d), seg),
            pl.BlockSpec((1, 1, d), seg),
            pl.BlockSpec((1, 1, d), seg),
            pl.BlockSpec((1, d), fix),
            pl.BlockSpec((1, d), fix),
            pl.BlockSpec((d, N_EXPERTS), fix),
            pl.BlockSpec((1, N_EXPERTS), fix),
        ],
        out_specs=[
            pl.BlockSpec((tm, d), row),
            pl.BlockSpec((tm, d), row),
            pl.BlockSpec((tm, N_EXPERTS), row),
        ],
        out_shape=[
            jax.ShapeDtypeStruct((m, d), F32),
            jax.ShapeDtypeStruct((m, d), F32),
            jax.ShapeDtypeStruct((m, N_EXPERTS), F32),
        ],
        compiler_params=_params(48, 1),
        name="oproj",
    )(a1, a2, wb, wb, x, gate, scale, shift, ln_g.reshape(1, d), ln_b.reshape(1, d),
      w_router.astype(BF16), b_router.reshape(1, N_EXPERTS))


def _rope_tables():
    t = jnp.arange(DEC_SEQ)
    rows = (t // GRID_W).astype(F32)
    cols = (t % GRID_W).astype(F32)
    n_freq = HEAD_DIM // 4
    inv = ROPE_THETA ** (-jnp.arange(n_freq, dtype=F32) / n_freq)
    ar = rows[:, None] * inv
    ac = cols[:, None] * inv
    zero = jnp.zeros_like(ar)
    c = jnp.concatenate([jnp.cos(ar), jnp.cos(ar), jnp.cos(ac), jnp.cos(ac)], -1)
    sa = jnp.concatenate([-jnp.sin(ar), zero, -jnp.sin(ac), zero], -1)
    sb = jnp.concatenate([zero, jnp.sin(ar), zero, jnp.sin(ac)], -1)
    return c, sa, sb


def _rope(x, c, sa, sb):
    return x * c + pltpu.roll(x, 96, 1) * sa + pltpu.roll(x, 32, 1) * sb


def _rms(x, g):
    return x * lax.rsqrt(jnp.mean(x * x, -1, keepdims=True) + NORM_EPS) * g


def _attn_kernel(*refs, T, G, tq, rope, ctx, window, sink, norm, emit_kn):
    it = iter(refs)
    q_ref, k_ref, v_ref = next(it), next(it), next(it)
    ck_ref = cv_ref = c_ref = sa_ref = sb_ref = sink_ref = qg_ref = kg_ref = kn_ref = None
    if ctx:
        ck_ref, cv_ref = next(it), next(it)
    if rope:
        c_ref, sa_ref, sb_ref = next(it), next(it), next(it)
    if sink:
        sink_ref = next(it)
    if norm:
        qg_ref, kg_ref = next(it), next(it)
    o_ref = next(it)
    if emit_kn:
        kn_ref = next(it)
    kr_ref, vr_ref = next(it), next(it)
    scale = HEAD_DIM ** -0.5
    kvh = pl.program_id(1)

    k = k_ref[...]
    if norm:
        k = _rms(k, kg_ref[...])
        if emit_kn:
            kn_ref[...] = k
    if rope:
        k = _rope(k, c_ref[...], sa_ref[...], sb_ref[...])
    kr_ref[...] = k.astype(BF16)
    vr_ref[...] = v_ref[...].astype(BF16)

    def block(i, carry):
        r0 = pl.multiple_of(i * tq, tq)
        if window:
            k0 = pl.multiple_of(jnp.clip(r0 - WINDOW, 0, T - 3 * WINDOW), WINDOW)
            nk = 3 * WINDOW
            kb = kr_ref[pl.ds(k0, nk), :]
            vb = vr_ref[pl.ds(k0, nk), :]
            qpos = r0 + lax.broadcasted_iota(jnp.int32, (tq, nk), 0)
            kpos = k0 + lax.broadcasted_iota(jnp.int32, (tq, nk), 1)
            keep = jnp.abs(qpos - kpos) <= WINDOW
        else:
            kb = kr_ref[...]
            vb = vr_ref[...]
        for h in range(G):
            q = q_ref[pl.ds(r0, tq), h * HEAD_DIM:(h + 1) * HEAD_DIM]
            if norm:
                q = _rms(q, qg_ref[...])
            if rope:
                q = _rope(q, c_ref[pl.ds(r0, tq), :], sa_ref[pl.ds(r0, tq), :], sb_ref[pl.ds(r0, tq), :])
            qb = q.astype(BF16)
            s = _dot_nt(qb, kb) * scale
            if window:
                s = jnp.where(keep, s, -1e30)
            m = jnp.max(s, -1, keepdims=True)
            if ctx:
                sc = _dot_nt(qb, ck_ref[...].astype(BF16)) * scale
                m = jnp.maximum(m, jnp.max(sc, -1, keepdims=True))
            if sink:
                sk = sink_ref[kvh * G + h]
                m = jnp.maximum(m, sk)
            p = jnp.exp(s - m)
            den = jnp.sum(p, -1, keepdims=True)
            o = _dot(p.astype(BF16), vb)
            if ctx:
                pc = jnp.exp(sc - m)
                den = den + jnp.sum(pc, -1, keepdims=True)
                o = o + _dot(pc.astype(BF16), cv_ref[...].astype(BF16))
            if sink:
                den = den + jnp.exp(sk - m)
            o_ref[pl.ds(r0, tq), h * HEAD_DIM:(h + 1) * HEAD_DIM] = (o / den).astype(o_ref.dtype)
        return carry

    lax.fori_loop(0, T // tq, block, 0)


def _attention(p, *, nb, T, row0, q_col, k_col, v_col, n_kv, G, tq, tabs=None, ctx=None,
               layer=0, window=False, sink=None, norm=None, emit_kn=False):
    gw = G * HEAD_DIM
    in_specs = [
        pl.BlockSpec((T, gw), lambda b, k: (row0 + b, q_col // gw + k)),
        pl.BlockSpec((T, HEAD_DIM), lambda b, k: (row0 + b, k_col // HEAD_DIM + k)),
        pl.BlockSpec((T, HEAD_DIM), lambda b, k: (row0 + b, v_col // HEAD_DIM + k)),
    ]
    args = [p, p, p]
    if ctx is not None:
        for a in ctx:
            args.append(a.reshape(a.shape[0], a.shape[1], PAST_LEN, n_kv * HEAD_DIM))
            in_specs.append(pl.BlockSpec((None, None, PAST_LEN, HEAD_DIM), lambda b, k: (b, layer, 0, k)))
    if tabs is not None:
        for a in tabs:
            args.append(a)
            in_specs.append(pl.BlockSpec((T, HEAD_DIM), lambda b, k: (0, 0)))
    if sink is not None:
        args.append(sink)
        in_specs.append(pl.BlockSpec(memory_space=pltpu.SMEM))
    if norm is not None:
        for a in norm:
            args.append(a.reshape(1, HEAD_DIM))
            in_specs.append(pl.BlockSpec((1, HEAD_DIM), lambda b, k: (0, 0)))
    out_specs = [pl.BlockSpec((T, gw), lambda b, k: (b, k))]
    out_shape = [jax.ShapeDtypeStruct((nb * T, n_kv * gw), BF16)]
    if emit_kn:
        out_specs.append(pl.BlockSpec((T, HEAD_DIM), lambda b, k: (b, k)))
        out_shape.append(jax.ShapeDtypeStruct((nb * T, n_kv * HEAD_DIM), F32))
    kern = functools.partial(
        _attn_kernel, T=T, G=G, tq=tq, rope=tabs is not None, ctx=ctx is not None,
        window=window, sink=sink is not None, norm=norm is not None, emit_kn=emit_kn)
    out = pl.pallas_call(
        kern,
        grid=(nb, n_kv),
        in_specs=in_specs,
        out_specs=out_specs,
        out_shape=out_shape,
        scratch_shapes=[pltpu.VMEM((T, HEAD_DIM), BF16), pltpu.VMEM((T, HEAD_DIM), BF16)],
        compiler_params=_params(40, 2),
        name="attn",
    )(*args)
    return out if emit_kn else out[0]


def _diff_kernel(*refs, T, tq, rope, ctx, lam_init):
    it = iter(refs)
    q_ref, k_ref, v_ref = next(it), next(it), next(it)
    ck_ref = cv_ref = c_ref = sa_ref = sb_ref = None
    if ctx:
        ck_ref, cv_ref = next(it), next(it)
    if rope:
        c_ref, sa_ref, sb_ref = next(it), next(it), next(it)
    lam_ref, sub_ref, o_ref, kr_ref, vr_ref = next(it), next(it), next(it), next(it), next(it)
    scale = HEAD_DIM ** -0.5
    D2 = 2 * HEAD_DIM

    for hh in range(2):
        k = k_ref[:, hh * HEAD_DIM:(hh + 1) * HEAD_DIM]
        if rope:
            k = _rope(k, c_ref[...], sa_ref[...], sb_ref[...])
        kr_ref[:, hh * HEAD_DIM:(hh + 1) * HEAD_DIM] = k.astype(BF16)
    vr_ref[...] = v_ref[...].astype(BF16)

    lam = lam_ref[...]
    lam_val = (jnp.exp(jnp.sum(lam[0:1] * lam[1:2], -1, keepdims=True))
               - jnp.exp(jnp.sum(lam[2:3] * lam[3:4], -1, keepdims=True)) + lam_init)

    def block(i, carry):
        r0 = pl.multiple_of(i * tq, tq)
        vb = vr_ref[...]
        outs = []
        for hh in range(2):
            cs = slice(hh * HEAD_DIM, (hh + 1) * HEAD_DIM)
            q = q_ref[pl.ds(r0, tq), cs]
            if rope:
                q = _rope(q, c_ref[pl.ds(r0, tq), :], sa_ref[pl.ds(r0, tq), :], sb_ref[pl.ds(r0, tq), :])
            qb = q.astype(BF16)
            s = _dot_nt(qb, kr_ref[:, cs]) * scale
            m = jnp.max(s, -1, keepdims=True)
            if ctx:
                sc = _dot_nt(qb, ck_ref[:, cs].astype(BF16)) * scale
                m = jnp.maximum(m, jnp.max(sc, -1, keepdims=True))
            p = jnp.exp(s - m)
            den = jnp.sum(p, -1, keepdims=True)
            o = _dot(p.astype(BF16), vb)
            if ctx:
                pc = jnp.exp(sc - m)
                den = den + jnp.sum(pc, -1, keepdims=True)
                o = o + _dot(pc.astype(BF16), cv_ref[...].astype(BF16))
            outs.append(o / den)
        d = outs[0] - lam_val * outs[1]
        o_ref[pl.ds(r0, tq), :] = (_rms(d, sub_ref[...]) * (1.0 - lam_init)).astype(o_ref.dtype)
        return carry

    lax.fori_loop(0, T // tq, block, 0)


def _diff_attention(p, *, nb, T, row0, tq, lam, subln, lam_init, tabs=None, ctx=None, layer=0):
    D2 = 2 * HEAD_DIM
    q_col, k_col, v_col = 5120, 6144, 7168
    in_specs = [
        pl.BlockSpec((T, D2), lambda b, h: (row0 + b, q_col // D2 + h)),
        pl.BlockSpec((T, D2), lambda b, h: (row0 + b, k_col // D2 + h)),
        pl.BlockSpec((T, D2), lambda b, h: (row0 + b, v_col // D2 + h)),
    ]
    args = [p, p, p]
    if ctx is not None:
        for a in ctx:
            args.append(a.reshape(a.shape[0], a.shape[1], PAST_LEN, H_D * D2))
            in_specs.append(pl.BlockSpec((None, None, PAST_LEN, D2), lambda b, h: (b, layer, 0, h)))
    if tabs is not None:
        for a in tabs:
            args.append(a)
            in_specs.append(pl.BlockSpec((T, HEAD_DIM), lambda b, h: (0, 0)))
    args += [lam, subln.reshape(1, D2)]
    in_specs += [pl.BlockSpec((4, HEAD_DIM), lambda b, h: (0, 0)),
                 pl.BlockSpec((1, D2), lambda b, h: (0, 0))]
    kern = functools.partial(_diff_kernel, T=T, tq=tq, rope=tabs is not None,
                             ctx=ctx is not None, lam_init=lam_init)
    return pl.pallas_call(
        kern,
        grid=(nb, H_D),
        in_specs=in_specs,
        out_specs=pl.BlockSpec((T, D2), lambda b, h: (b, h)),
        out_shape=jax.ShapeDtypeStruct((nb * T, H_D * D2), BF16),
        scratch_shapes=[pltpu.VMEM((T, D2), BF16), pltpu.VMEM((T, D2), BF16)],
        compiler_params=_params(40, 2),
        name="diff_attn",
    )(*args)


def _head_norm(x):
    mu = jnp.mean(x, -1, keepdims=True)
    xc = x - mu
    var = jnp.mean(xc * xc, -1, keepdims=True)
    return xc * lax.rsqrt(var + NORM_EPS)


def _silu(x):
    return x * jax.nn.sigmoid(x)


def _ret_kernel(*refs, T, rope, has_state):
    it = iter(refs)
    q_ref, k_ref, v_ref, gf_ref, gb_ref, dec_ref = (next(it) for _ in range(6))
    s0f_ref = s0b_ref = c_ref = sa_ref = sb_ref = None
    if has_state:
        s0f_ref, s0b_ref = next(it), next(it)
    if rope:
        c_ref, sa_ref, sb_ref = next(it), next(it), next(it)
    o_ref, sf_ref, sb_out_ref, acc_ref = next(it), next(it), next(it), next(it)
    C = RET_CHUNK
    n = T // C
    head = pl.program_id(1)
    ri = lax.broadcasted_iota(jnp.int32, (C, C), 0).astype(F32)
    ci = lax.broadcasted_iota(jnp.int32, (C, C), 1).astype(F32)

    def log_gamma(d):
        x = jnp.full((1, C), dec_ref[d, head], F32)
        return jnp.minimum(x, 0.0) - jnp.log1p(jnp.exp(-jnp.abs(x)))

    def chunk(c0):
        rows = pl.ds(c0, C)
        q = q_ref[rows, :]
        k = k_ref[rows, :] * (HEAD_DIM ** -0.5)
        if rope:
            tab = (c_ref[rows, :], sa_ref[rows, :], sb_ref[rows, :])
            q = _rope(q, *tab)
            k = _rope(k, *tab)
        return q, k, v_ref[rows, :].astype(BF16)

    def step(q, k, vb, s, dmat, xi, zeta, g_chunk):
        att = _dot_nt(q.astype(BF16), k.astype(BF16)) * dmat
        o = _dot(att.astype(BF16), vb) + _dot((q * xi).astype(BF16), s.astype(BF16))
        s = g_chunk * s + _dot((k * zeta).T.astype(BF16), vb)
        return o, s

    lg = log_gamma(0)
    diff = ri - ci
    dmat = jnp.where(diff >= 0, jnp.exp(jnp.maximum(diff, 0.0) * lg), 0.0)
    xi = jnp.exp((ri + 1.0) * lg)
    zeta = jnp.exp((C - 1.0 - ri) * lg)
    g_chunk = jnp.exp(C * lg)
    s = s0f_ref[...] if has_state else jnp.zeros((C, C), F32)
    for c in range(n):
        q, k, vb = chunk(c * C)
        o, s = step(q, k, vb, s, dmat, xi, zeta, g_chunk)
        acc_ref[pl.ds(c * C, C), :] = _silu(gf_ref[pl.ds(c * C, C), :]) * _head_norm(o)
    sf_ref[...] = s

    lg = log_gamma(1)
    diff = ci - ri
    dmat = jnp.where(diff >= 0, jnp.exp(jnp.maximum(diff, 0.0) * lg), 0.0)
    xi = jnp.exp((C - ri) * lg)
    zeta = jnp.exp(ri * lg)
    g_chunk = jnp.exp(C * lg)
    s = s0b_ref[...] if has_state else jnp.zeros((C, C), F32)
    for c in reversed(range(n)):
        q, k, vb = chunk(c * C)
        o, s = step(q, k, vb, s, dmat, xi, zeta, g_chunk)
        y = acc_ref[pl.ds(c * C, C), :] + _silu(gb_ref[pl.ds(c * C, C), :]) * _head_norm(o)
        o_ref[pl.ds(c * C, C), :] = y.astype(o_ref.dtype)
    sb_out_ref[...] = s


def _retention(p, *, nb, T, row0, decay, tabs=None, state=None, layer=0):
    hd = HEAD_DIM
    col = lambda c0: (lambda b, h: (row0 + b, c0 // hd + h))
    in_specs = [pl.BlockSpec((T, hd), col(c0)) for c0 in (0, 1024, 2048, 3072, 4096)]
    args = [p] * 5 + [decay]
    in_specs.append(pl.BlockSpec(memory_space=pltpu.SMEM))
    if state is not None:
        for a in state:
            args.append(a)
            in_specs.append(pl.BlockSpec((None, None, None, hd, hd), lambda b, h: (b, layer, h, 0, 0)))
    if tabs is not None:
        for a in tabs:
            args.append(a)
            in_specs.append(pl.BlockSpec((T, hd), lambda b, h: (0, 0)))
    kern = functools.partial(_ret_kernel, T=T, rope=tabs is not None, has_state=state is not None)
    st_spec = pl.BlockSpec((None, None, hd, hd), lambda b, h: (b, h, 0, 0))
    return pl.pallas_call(
        kern,
        grid=(nb, H_C),
        in_specs=in_specs,
        out_specs=[pl.BlockSpec((T, hd), lambda b, h: (b, h)), st_spec, st_spec],
        out_shape=[jax.ShapeDtypeStruct((nb * T, H_C * hd), BF16),
                   jax.ShapeDtypeStruct((nb, H_C, hd, hd), F32),
                   jax.ShapeDtypeStruct((nb, H_C, hd, hd), F32)],
        scratch_shapes=[pltpu.VMEM((T, hd), F32)],
        compiler_params=_params(40, 2),
        name="retention",
    )(*args)


def _route(logits):
    top_v, top_e = lax.top_k(logits, TOP_K)
    gates = jax.nn.softmax(top_v, axis=-1)
    chosen = jnp.sum((top_e[:, :, None] == jnp.arange(N_EXPERTS)[None, None, :]).astype(jnp.int32), 1)
    rank_all = jnp.cumsum(chosen, 0) - chosen
    counts = jnp.sum(chosen, 0)
    padded = (counts + MOE_TM - 1) // MOE_TM * MOE_TM
    ends_p = jnp.cumsum(padded)
    start_p = ends_p - padded
    dest = start_p[top_e] + jnp.take_along_axis(rank_all, top_e, 1)
    flat = dest.reshape(-1)
    tok = jnp.repeat(jnp.arange(N_TOK, dtype=jnp.int32), TOP_K)
    row_tok = jnp.zeros((MOE_ROWS,), jnp.int32).at[flat].set(tok)
    row_w = jnp.zeros((MOE_ROWS,), F32).at[flat].set(gates.reshape(-1))
    block_e = jnp.minimum(
        jnp.searchsorted(ends_p, jnp.arange(MOE_BLOCKS, dtype=jnp.int32) * MOE_TM, side='right'),
        N_EXPERTS - 1).astype(jnp.int32)
    n_used = (ends_p[-1] // MOE_TM).astype(jnp.int32).reshape(1)
    return dest.astype(jnp.int32), row_tok, row_w, block_e, n_used


GATHER_R = 256


def _gather_kernel(tok_ref, h_hbm, o_ref, sem):
    def copy(r):
        return pltpu.make_async_copy(h_hbm.at[pl.ds(tok_ref[0, 0, r], 1), :],
                                     o_ref.at[pl.ds(r, 1), :], sem)

    def start(r, c):
        copy(r).start()
        return c

    def wait(r, c):
        copy(r).wait()
        return c

    lax.fori_loop(0, GATHER_R, start, 0)
    lax.fori_loop(0, GATHER_R, wait, 0)


def _gather_rows(h, row_tok):
    steps = MOE_ROWS // GATHER_R
    return pl.pallas_call(
        _gather_kernel,
        grid=(steps,),
        in_specs=[
            pl.BlockSpec((1, 1, GATHER_R), lambda i: (i, 0, 0), memory_space=pltpu.SMEM),
            pl.BlockSpec(memory_space=pl.ANY),
        ],
        out_specs=pl.BlockSpec((GATHER_R, D_MODEL), lambda i: (i, 0)),
        out_shape=jax.ShapeDtypeStruct((MOE_ROWS, D_MODEL), F32),
        scratch_shapes=[pltpu.SemaphoreType.DMA(())],
        compiler_params=_params(32, 1),
        name="moe_gather",
    )(row_tok.reshape(steps, 1, GATHER_R), h)


def _new_expert(be_ref, m):
    prev = be_ref[jnp.maximum(m - 1, 0)]
    return jnp.logical_or(m == 0, be_ref[m] != prev)


def _gate_up_kernel(be_ref, nu_ref, x_ref, wg_ref, bg_ref, wu_ref, bu_ref, o_ref, wgb_ref, wub_ref):
    m = pl.program_id(1)

    @pl.when(_new_expert(be_ref, m))
    def _():
        wgb_ref[...] = wg_ref[0].astype(BF16)
        wub_ref[...] = wu_ref[0].astype(BF16)

    @pl.when(m < nu_ref[0])
    def _():
        xb = x_ref[...].astype(BF16)
        g = jnp.minimum(_dot(xb, wgb_ref[...]) + bg_ref[0], SWIGLU_LIMIT)
        u = jnp.clip(_dot(xb, wub_ref[...]) + bu_ref[0], -SWIGLU_LIMIT, SWIGLU_LIMIT)
        o_ref[...] = (g * jax.nn.sigmoid(SWIGLU_ALPHA * g) * (u + 1.0)).astype(o_ref.dtype)

    @pl.when(m >= nu_ref[0])
    def _():
        o_ref[...] = jnp.zeros_like(o_ref)


def _gate_up(xs, block_e, n_used, w_gate, b_gate, w_up, b_up):
    tm, tn = MOE_TM, MOE_TN
    wspec = pl.BlockSpec((1, D_MODEL, tn), lambda j, i, be, nu: (be[i], 0, j))
    bspec = pl.BlockSpec((1, 1, tn), lambda j, i, be, nu: (be[i], 0, j))
    return pl.pallas_call(
        _gate_up_kernel,
        grid_spec=pltpu.PrefetchScalarGridSpec(
            num_scalar_prefetch=2,
            grid=(D_FF // tn, MOE_BLOCKS),
            in_specs=[pl.BlockSpec((tm, D_MODEL), lambda j, i, be, nu: (i, 0)),
                      wspec, bspec, wspec, bspec],
            out_specs=pl.BlockSpec((tm, tn), lambda j, i, be, nu: (i, j)),
            scratch_shapes=[pltpu.VMEM((D_MODEL, tn), BF16), pltpu.VMEM((D_MODEL, tn), BF16)],
        ),
        out_shape=jax.ShapeDtypeStruct((MOE_ROWS, D_FF), BF16),
        compiler_params=_params(48, 2),
        name="moe_gate_up",
    )(block_e, n_used, xs, w_gate, b_gate.reshape(N_EXPERTS, 1, D_FF), w_up, b_up.reshape(N_EXPERTS, 1, D_FF))


def _down_kernel(be_ref, nu_ref, h_ref, wd_ref, bd_ref, rw_ref, o_ref, wdb_ref):
    m = pl.program_id(1)

    @pl.when(_new_expert(be_ref, m))
    def _():
        wdb_ref[...] = wd_ref[0].astype(BF16)

    @pl.when(m < nu_ref[0])
    def _():
        o_ref[...] = (_dot(h_ref[...], wdb_ref[...]) + bd_ref[0]) * rw_ref[...]

    @pl.when(m >= nu_ref[0])
    def _():
        o_ref[...] = jnp.zeros_like(o_ref)


def _down(hs, row_w, block_e, n_used, w_down, b_down):
    tm, tn = MOE_TM, MOE_TN
    return pl.pallas_call(
        _down_kernel,
        grid_spec=pltpu.PrefetchScalarGridSpec(
            num_scalar_prefetch=2,
            grid=(D_MODEL // tn, MOE_BLOCKS),
            in_specs=[pl.BlockSpec((tm, D_FF), lambda j, i, be, nu: (i, 0)),
                      pl.BlockSpec((1, D_FF, tn), lambda j, i, be, nu: (be[i], 0, j)),
                      pl.BlockSpec((1, 1, tn), lambda j, i, be, nu: (be[i], 0, j)),
                      pl.BlockSpec((tm, 1), lambda j, i, be, nu: (i, 0))],
            out_specs=pl.BlockSpec((tm, tn), lambda j, i, be, nu: (i, j)),
            scratch_shapes=[pltpu.VMEM((D_FF, tn), BF16)],
        ),
        out_shape=jax.ShapeDtypeStruct((MOE_ROWS, D_MODEL), F32),
        compiler_params=_params(48, 2),
        name="moe_down",
    )(block_e, n_used, hs, w_down, b_down.reshape(N_EXPERTS, 1, D_MODEL), row_w.reshape(MOE_ROWS, 1))


COMBINE_T = 128


def _combine_kernel(dest_ref, ys_hbm, x_ref, gate_ref, lng_ref, lnb_ref, xo_ref, buf_ref, sem):
    n = COMBINE_T * TOP_K

    def copy(a):
        return pltpu.make_async_copy(ys_hbm.at[pl.ds(dest_ref[0, 0, a], 1), :],
                                     buf_ref.at[a % TOP_K, pl.ds(a // TOP_K, 1), :], sem)

    def start(a, c):
        copy(a).start()
        return c

    def wait(a, c):
        copy(a).wait()
        return c

    lax.fori_loop(0, n, start, 0)
    lax.fori_loop(0, n, wait, 0)
    f = (buf_ref[0] + buf_ref[1]) + (buf_ref[2] + buf_ref[3])
    z = DEEPNORM_ALPHA * x_ref[...] + gate_ref[0] * f
    xo_ref[...] = _layer_norm(z, lng_ref[...], lnb_ref[...])


def _combine(ys, dest, x, gate, ln_g, ln_b):
    t = COMBINE_T
    steps = N_TOK // t
    d = D_MODEL
    return pl.pallas_call(
        _combine_kernel,
        grid=(steps,),
        in_specs=[
            pl.BlockSpec((1, 1, t * TOP_K), lambda i: (i, 0, 0), memory_space=pltpu.SMEM),
            pl.BlockSpec(memory_space=pl.ANY),
            pl.BlockSpec((t, d), lambda i: (i, 0)),
            pl.BlockSpec((1, 1, d), lambda i: (i * t // SEG, 0, 0)),
            pl.BlockSpec((1, d), lambda i: (0, 0)),
            pl.BlockSpec((1, d), lambda i: (0, 0)),
        ],
        out_specs=pl.BlockSpec((t, d), lambda i: (i, 0)),
        out_shape=jax.ShapeDtypeStruct((N_TOK, d), F32),
        scratch_shapes=[pltpu.VMEM((TOP_K, t, d), F32), pltpu.SemaphoreType.DMA(())],
        compiler_params=_params(32, 1),
        name="moe_combine",
    )(dest.reshape(steps, 1, t * TOP_K), ys, x, gate, ln_g.reshape(1, d), ln_b.reshape(1, d))


def _moe(h, logits, x, gate, ln_g, ln_b, w_gate, b_gate, w_up, b_up, w_down, b_down):
    dest, row_tok, row_w, block_e, n_used = _route(logits)
    xs = _gather_rows(h, row_tok)
    hs = _gate_up(xs, block_e, n_used, w_gate, b_gate, w_up, b_up)
    ys = _down(hs, row_w, block_e, n_used, w_down, b_down)
    return _combine(ys, dest, x, gate, ln_g, ln_b)


def _even_mixer(p, i, tabs, sink, qg, kg, cache_k_a, cache_v_a, cache_k_b, cache_v_b):
    pb, sb = N_PROMPT // SEQ, N_PROMPT // DEC_SEQ
    norm = (qg, kg)
    oa_p = _attention(p, nb=BATCH, T=SEQ, row0=0, q_col=0, k_col=1024, v_col=1280,
                      n_kv=KV_A, G=H_A // KV_A, tq=SEQ, sink=sink)
    ob_p, kb_n = _attention(p, nb=BATCH, T=SEQ, row0=0, q_col=1536, k_col=2560, v_col=2816,
                            n_kv=KV_B, G=H_B // KV_B, tq=SEQ, norm=norm, emit_kn=True)
    oa_s = _attention(p, nb=DEC_BATCH, T=DEC_SEQ, row0=sb, q_col=0, k_col=1024, v_col=1280,
                      n_kv=KV_A, G=H_A // KV_A, tq=WINDOW, tabs=tabs, ctx=(cache_k_a, cache_v_a),
                      layer=i, window=True, sink=sink)
    ob_s = _attention(p, nb=DEC_BATCH, T=DEC_SEQ, row0=sb, q_col=1536, k_col=2560, v_col=2816,
                      n_kv=KV_B, G=H_B // KV_B, tq=256, tabs=tabs, ctx=(cache_k_b, cache_v_b),
                      layer=i, norm=norm)
    del pb
    pp = p[:N_PROMPT]
    ka = pp[:, 1024:1280].reshape(BATCH, SEQ, KV_A, HEAD_DIM)
    va = pp[:, 1280:1536].reshape(BATCH, SEQ, KV_A, HEAD_DIM)
    kb = kb_n.reshape(BATCH, SEQ, KV_B, HEAD_DIM)
    vb = pp[:, 2816:3072].reshape(BATCH, SEQ, KV_B, HEAD_DIM)
    return (jnp.concatenate([oa_p, oa_s], 0), jnp.concatenate([ob_p, ob_s], 0), (ka, va, kb, vb))


def _odd_mixer(p, i, layer, tabs, decay, lam, subln, state_f, state_b, cache_k_d, cache_v_d):
    sb = N_PROMPT // DEC_SEQ
    lam_init = 0.8 - 0.6 * math.exp(-0.3 * layer)
    yc_p, sf, sbk = _retention(p, nb=BATCH, T=SEQ, row0=0, decay=decay)
    yc_s, _, _ = _retention(p, nb=DEC_BATCH, T=DEC_SEQ, row0=sb, decay=decay, tabs=tabs,
                            state=(state_f, state_b), layer=i)
    yd_p = _diff_attention(p, nb=BATCH, T=SEQ, row0=0, tq=SEQ, lam=lam, subln=subln, lam_init=lam_init)
    yd_s = _diff_attention(p, nb=DEC_BATCH, T=DEC_SEQ, row0=sb, tq=256, lam=lam, subln=subln,
                           lam_init=lam_init, tabs=tabs, ctx=(cache_k_d, cache_v_d), layer=i)
    pp = p[:N_PROMPT]
    kd = pp[:, 6144:7168].reshape(BATCH, SEQ, H_D, 2 * HEAD_DIM)
    vd = pp[:, 7168:8192].reshape(BATCH, SEQ, H_D, 2 * HEAD_DIM)
    return (jnp.concatenate([yc_p, yc_s], 0), jnp.concatenate([yd_p, yd_s], 0), (sf, sbk, kd, vd))


def kernel(x_prompt, x_sample, c, cache_k_a, cache_v_a, cache_k_b, cache_v_b, state_ret_fwd, state_ret_bwd, cache_k_d, cache_v_d, c_ctx, w_ada, b_ada, ln_g, ln_b, w_in_ab, w_out_ab, sink_a, qnorm_b, knorm_b, w_in_cd, w_out_cd, ret_decay, lam_d, subln_d, w_router, b_router, w_gate, b_gate, w_up, b_up, w_down, b_down):
    x = jnp.concatenate([x_prompt.reshape(N_PROMPT, D_MODEL), x_sample.reshape(N_SAMPLE, D_MODEL)], 0)
    cond = jnp.concatenate([c_ctx[None], c, jnp.zeros((16 - 1 - DEC_BATCH, D_MODEL), F32)], 0)
    mods = _ada(cond, w_ada, b_ada)
    seg_row = jnp.array([0] * (N_PROMPT // SEG) + list(range(1, DEC_BATCH + 1)), jnp.int32)
    mods = mods[:, seg_row].reshape(DEPTH, N_SEG, 6, 1, D_MODEL)
    tabs = _rope_tables()
    caches = ([], [], [], [], [], [], [], [])
    for l in range(DEPTH):
        i = l // 2
        mod = [mods[l, :, j] for j in range(6)]
        if l % 2 == 0:
            p = _proj(x, mod[1], mod[0], w_in_ab[i])
            a1, a2, new = _even_mixer(p, i, tabs, sink_a[i], qnorm_b[i], knorm_b[i],
                                      cache_k_a, cache_v_a, cache_k_b, cache_v_b)
            w_out = w_out_ab[i]
            for dst, val in zip(caches[:4], new):
                dst.append(val)
        else:
            p = _proj(x, mod[1], mod[0], w_in_cd[i])
            a1, a2, new = _odd_mixer(p, i, l, tabs, ret_decay[i], lam_d[i], subln_d[i],
                                     state_ret_fwd, state_ret_bwd, cache_k_d, cache_v_d)
            w_out = w_out_cd[i]
            for dst, val in zip(caches[4:], new):
                dst.append(val)
        x, h, logits = _oproj(a1, a2, w_out, x, mod[2], mod[4], mod[3], ln_g[l, 0], ln_b[l, 0],
                              w_router[l], b_router[l])
        x = _moe(h, logits, x, mod[5], ln_g[l, 1], ln_b[l, 1],
                 w_gate[l], b_gate[l], w_up[l], b_up[l], w_down[l], b_down[l])
    xp = x[:N_PROMPT].reshape(BATCH, SEQ, D_MODEL)
    xs = x[N_PROMPT:].reshape(DEC_BATCH, DEC_SEQ, D_MODEL)
    return (xp, xs) + tuple(jnp.stack(v, 1) for v in caches)
```

```python
import functools
import math

import jax
import jax.numpy as jnp
from jax import lax
from jax.experimental import pallas as pl
from jax.experimental.pallas import tpu as pltpu

F32 = jnp.float32
BF16 = jnp.bfloat16

D_MODEL = 2048
BATCH = 16
SEQ = 256
DEPTH = 4
DEC_BATCH = 8
DEC_SEQ = 1024
PAST_LEN = 256
GRID_W = 64
HEAD_DIM = 128
H_A = 8
KV_A = 2
WINDOW = 128
H_B = 8
KV_B = 2
H_C = 8
H_D = 4
RET_CHUNK = 128
ROPE_THETA = 10000.0
N_EXPERTS = 32
TOP_K = 4
D_FF = 2048
SWIGLU_LIMIT = 7.0
SWIGLU_ALPHA = 1.702
DEEPNORM_ALPHA = (2 * DEPTH) ** 0.25
NORM_EPS = 1e-6

N_PROMPT = BATCH * SEQ
N_SAMPLE = DEC_BATCH * DEC_SEQ
N_TOK = N_PROMPT + N_SAMPLE
SEG = DEC_SEQ
N_SEG = N_TOK // SEG
EVEN_IN = 3072
ODD_IN = 8192

MOE_TM = 256
MOE_TN = 1024
LANES = 128
CHUNKS = D_MODEL // LANES
N_ASSIGN = N_TOK * TOP_K
MOE_BLOCKS = (N_ASSIGN + N_EXPERTS * (MOE_TM - 1) + MOE_TM - 1) // MOE_TM
MOE_ROWS = MOE_BLOCKS * MOE_TM

MIB = 1024 * 1024


def _params(vmem_mib, n_axes):
    return pltpu.CompilerParams(
        dimension_semantics=("arbitrary",) * n_axes,
        vmem_limit_bytes=vmem_mib * MIB)


def _dot(a, b):
    return jnp.dot(a, b, preferred_element_type=F32)


def _dot_nt(a, b):
    return lax.dot_general(a, b, (((1,), (1,)), ((), ())), preferred_element_type=F32)


def _layer_norm(z, g, b):
    mu = jnp.mean(z, -1, keepdims=True)
    zc = z - mu
    var = jnp.mean(zc * zc, -1, keepdims=True)
    return zc * lax.rsqrt(var + NORM_EPS) * g + b


def _ada_kernel(c_ref, w_ref, b_ref, o_ref):
    c = c_ref[...]
    a = c * jax.nn.sigmoid(c)
    o_ref[0] = _dot(a.astype(BF16), w_ref[0].astype(BF16)) + b_ref[0]


def _ada(cond, w_ada, b_ada):
    tn = 1024
    n = w_ada.shape[-1]
    return pl.pallas_call(
        _ada_kernel,
        grid=(DEPTH, n // tn),
        in_specs=[
            pl.BlockSpec((16, D_MODEL), lambda l, j: (0, 0)),
            pl.BlockSpec((1, D_MODEL, tn), lambda l, j: (l, 0, j)),
            pl.BlockSpec((1, 1, tn), lambda l, j: (l, 0, j)),
        ],
        out_specs=pl.BlockSpec((1, 16, tn), lambda l, j: (l, 0, j)),
        out_shape=jax.ShapeDtypeStruct((DEPTH, 16, n), F32),
        compiler_params=_params(40, 2),
        name="ada",
    )(cond, w_ada, b_ada.reshape(DEPTH, 1, n))


def _proj_kernel(x_ref, sc_ref, sh_ref, w_ref, o_ref, wb_ref):
    @pl.when(pl.program_id(1) == 0)
    def _():
        wb_ref[...] = w_ref[...].astype(BF16)

    h = x_ref[...] * (1.0 + sc_ref[0]) + sh_ref[0]
    o_ref[...] = _dot(h.astype(BF16), wb_ref[...])


def _proj(x, scale, shift, w, layer):
    tm, tn = 512, 1024
    m, k = x.shape
    n = w.shape[2]
    return pl.pallas_call(
        _proj_kernel,
        grid=(n // tn, m // tm),
        in_specs=[
            pl.BlockSpec((tm, k), lambda j, i: (i, 0)),
            pl.BlockSpec((1, 1, k), lambda j, i: (i * tm // SEG, 0, 0)),
            pl.BlockSpec((1, 1, k), lambda j, i: (i * tm // SEG, 0, 0)),
            pl.BlockSpec((None, k, tn), lambda j, i: (layer, 0, j)),
        ],
        out_specs=pl.BlockSpec((tm, tn), lambda j, i: (i, j)),
        out_shape=jax.ShapeDtypeStruct((m, n), F32),
        scratch_shapes=[pltpu.VMEM((k, tn), BF16)],
        compiler_params=_params(48, 2),
        name="proj",
    )(x, scale, shift, w)


def _store_chunked(ref, val, n):
    for j in range(CHUNKS):
        ref[pl.ds(j, n, stride=CHUNKS), :] = val[:, j * LANES:(j + 1) * LANES]


def _load_chunk(ref, j, n):
    return ref[pl.ds(j, n, stride=CHUNKS), :]


def _oproj_kernel(a1_ref, a2_ref, w1_ref, w2_ref, x_ref, gate_ref, sc_ref, sh_ref,
                  lng_ref, lnb_ref, wr_ref, br_ref, xo_ref, h_ref, lg_ref):
    y = _dot(a1_ref[...], w1_ref[...]) + _dot(a2_ref[...], w2_ref[...])
    z = DEEPNORM_ALPHA * x_ref[...] + gate_ref[0] * y
    xn = _layer_norm(z, lng_ref[...], lnb_ref[...])
    xo_ref[...] = xn
    h = xn * (1.0 + sc_ref[0]) + sh_ref[0]
    _store_chunked(h_ref, h, h.shape[0])
    lg_ref[...] = _dot(h.astype(BF16), wr_ref[...]) + br_ref[...]


def _oproj(a1, a2, w_out, x, gate, scale, shift, ln_g, ln_b, w_router, b_router):
    tm = 256
    m, d = x.shape
    half = a1.shape[1]
    wb = w_out.astype(BF16)
    seg = lambda i: (i * tm // SEG, 0, 0)
    row = lambda i: (i, 0)
    fix = lambda i: (0, 0)
    return pl.pallas_call(
        _oproj_kernel,
        grid=(m // tm,),
        in_specs=[
            pl.BlockSpec((tm, half), row),
            pl.BlockSpec((tm, half), row),
            pl.BlockSpec((half, d), fix),
            pl.BlockSpec((half, d), lambda i: (1, 0)),
            pl.BlockSpec((tm, d), row),
            pl.BlockSpec((1, 1, d), seg),
            pl.BlockSpec((1, 1, d), seg),
            pl.BlockSpec((1, 1, d), seg),
            pl.BlockSpec((1, d), fix),
            pl.BlockSpec((1, d), fix),
            pl.BlockSpec((d, N_EXPERTS), fix),
            pl.BlockSpec((1, N_EXPERTS), fix),
        ],
        out_specs=[
            pl.BlockSpec((tm, d), row),
            pl.BlockSpec((tm * CHUNKS, LANES), row),
            pl.BlockSpec((tm, N_EXPERTS), row),
        ],
        out_shape=[
            jax.ShapeDtypeStruct((m, d), F32),
            jax.ShapeDtypeStruct((m * CHUNKS, LANES), F32),
            jax.ShapeDtypeStruct((m, N_EXPERTS), F32),
        ],
        compiler_params=_params(48, 1),
        name="oproj",
    )(a1, a2, wb, wb, x, gate, scale, shift, ln_g.reshape(1, d), ln_b.reshape(1, d),
      w_router.astype(BF16), b_router.reshape(1, N_EXPERTS))


def _rope_tables():
    t = jnp.arange(DEC_SEQ)
    rows = (t // GRID_W).astype(F32)
    cols = (t % GRID_W).astype(F32)
    n_freq = HEAD_DIM // 4
    inv = ROPE_THETA ** (-jnp.arange(n_freq, dtype=F32) / n_freq)
    ar = rows[:, None] * inv
    ac = cols[:, None] * inv
    zero = jnp.zeros_like(ar)
    c = jnp.concatenate([jnp.cos(ar), jnp.cos(ar), jnp.cos(ac), jnp.cos(ac)], -1)
    sa = jnp.concatenate([-jnp.sin(ar), zero, -jnp.sin(ac), zero], -1)
    sb = jnp.concatenate([zero, jnp.sin(ar), zero, jnp.sin(ac)], -1)
    return c, sa, sb


def _rope(x, c, sa, sb):
    return x * c + pltpu.roll(x, 96, 1) * sa + pltpu.roll(x, 32, 1) * sb


def _rms(x, g):
    return x * lax.rsqrt(jnp.mean(x * x, -1, keepdims=True) + NORM_EPS) * g


def _attn_kernel(*refs, T, G, tq, rope, ctx, window, sink, norm, emit_kn):
    it = iter(refs)
    q_ref, k_ref, v_ref = next(it), next(it), next(it)
    ck_ref = cv_ref = c_ref = sa_ref = sb_ref = sink_ref = qg_ref = kg_ref = kn_ref = None
    if ctx:
        ck_ref, cv_ref = next(it), next(it)
    if rope:
        c_ref, sa_ref, sb_ref = next(it), next(it), next(it)
    if sink:
        sink_ref = next(it)
    if norm:
        qg_ref, kg_ref = next(it), next(it)
    o_ref = next(it)
    if emit_kn:
        kn_ref = next(it)
    kr_ref, vr_ref = next(it), next(it)
    scale = HEAD_DIM ** -0.5
    kvh = pl.program_id(1)

    k = k_ref[...]
    if norm:
        k = _rms(k, kg_ref[...])
        if emit_kn:
            kn_ref[...] = k
    if rope:
        k = _rope(k, c_ref[...], sa_ref[...], sb_ref[...])
    kr_ref[...] = k.astype(BF16)
    vr_ref[...] = v_ref[...].astype(BF16)

    def block(i, carry):
        r0 = pl.multiple_of(i * tq, tq)
        if window:
            k0 = pl.multiple_of(jnp.clip(r0 - WINDOW, 0, T - 3 * WINDOW), WINDOW)
            nk = 3 * WINDOW
            kb = kr_ref[pl.ds(k0, nk), :]
            vb = vr_ref[pl.ds(k0, nk), :]
            qpos = r0 + lax.broadcasted_iota(jnp.int32, (tq, nk), 0)
            kpos = k0 + lax.broadcasted_iota(jnp.int32, (tq, nk), 1)
            keep = jnp.abs(qpos - kpos) <= WINDOW
        else:
            kb = kr_ref[...]
            vb = vr_ref[...]
        for h in range(G):
            q = q_ref[pl.ds(r0, tq), h * HEAD_DIM:(h + 1) * HEAD_DIM]
            if norm:
                q = _rms(q, qg_ref[...])
            if rope:
                q = _rope(q, c_ref[pl.ds(r0, tq), :], sa_ref[pl.ds(r0, tq), :], sb_ref[pl.ds(r0, tq), :])
            qb = q.astype(BF16)
            s = _dot_nt(qb, kb) * scale
            if window:
                s = jnp.where(keep, s, -1e30)
            m = jnp.max(s, -1, keepdims=True)
            if ctx:
                sc = _dot_nt(qb, ck_ref[...].astype(BF16)) * scale
                m = jnp.maximum(m, jnp.max(sc, -1, keepdims=True))
            if sink:
                sk = sink_ref[kvh * G + h]
                m = jnp.maximum(m, sk)
            p = jnp.exp(s - m)
            den = jnp.sum(p, -1, keepdims=True)
            o = _dot(p.astype(BF16), vb)
            if ctx:
                pc = jnp.exp(sc - m)
                den = den + jnp.sum(pc, -1, keepdims=True)
                o = o + _dot(pc.astype(BF16), cv_ref[...].astype(BF16))
            if sink:
                den = den + jnp.exp(sk - m)
            o_ref[pl.ds(r0, tq), h * HEAD_DIM:(h + 1) * HEAD_DIM] = (o / den).astype(o_ref.dtype)
        return carry

    lax.fori_loop(0, T // tq, block, 0)


def _attention(p, *, nb, T, row0, q_col, k_col, v_col, n_kv, G, tq, tabs=None, ctx=None,
               layer=0, window=False, sink=None, norm=None, emit_kn=False):
    gw = G * HEAD_DIM
    in_specs = [
        pl.BlockSpec((T, gw), lambda b, k: (row0 + b, q_col // gw + k)),
        pl.BlockSpec((T, HEAD_DIM), lambda b, k: (row0 + b, k_col // HEAD_DIM + k)),
        pl.BlockSpec((T, HEAD_DIM), lambda b, k: (row0 + b, v_col // HEAD_DIM + k)),
    ]
    args = [p, p, p]
    if ctx is not None:
        for a in ctx:
            args.append(a.reshape(a.shape[0], a.shape[1], PAST_LEN, n_kv * HEAD_DIM))
            in_specs.append(pl.BlockSpec((None, None, PAST_LEN, HEAD_DIM), lambda b, k: (b, layer, 0, k)))
    if tabs is not None:
        for a in tabs:
            args.append(a)
            in_specs.append(pl.BlockSpec((T, HEAD_DIM), lambda b, k: (0, 0)))
    if sink is not None:
        args.append(sink)
        in_specs.append(pl.BlockSpec(memory_space=pltpu.SMEM))
    if norm is not None:
        for a in norm:
            args.append(a.reshape(1, HEAD_DIM))
            in_specs.append(pl.BlockSpec((1, HEAD_DIM), lambda b, k: (0, 0)))
    out_specs = [pl.BlockSpec((T, gw), lambda b, k: (b, k))]
    out_shape = [jax.ShapeDtypeStruct((nb * T, n_kv * gw), BF16)]
    if emit_kn:
        out_specs.append(pl.BlockSpec((T, HEAD_DIM), lambda b, k: (b, k)))
        out_shape.append(jax.ShapeDtypeStruct((nb * T, n_kv * HEAD_DIM), F32))
    kern = functools.partial(
        _attn_kernel, T=T, G=G, tq=tq, rope=tabs is not None, ctx=ctx is not None,
        window=window, sink=sink is not None, norm=norm is not None, emit_kn=emit_kn)
    out = pl.pallas_call(
        kern,
        grid=(nb, n_kv),
        in_specs=in_specs,
        out_specs=out_specs,
        out_shape=out_shape,
        scratch_shapes=[pltpu.VMEM((T, HEAD_DIM), BF16), pltpu.VMEM((T, HEAD_DIM), BF16)],
        compiler_params=_params(40, 2),
        name="attn",
    )(*args)
    return out if emit_kn else out[0]


def _diff_kernel(*refs, T, tq, rope, ctx, lam_init):
    it = iter(refs)
    q_ref, k_ref, v_ref = next(it), next(it), next(it)
    ck_ref = cv_ref = c_ref = sa_ref = sb_ref = None
    if ctx:
        ck_ref, cv_ref = next(it), next(it)
    if rope:
        c_ref, sa_ref, sb_ref = next(it), next(it), next(it)
    lam_ref, sub_ref, o_ref, kr_ref, vr_ref = next(it), next(it), next(it), next(it), next(it)
    scale = HEAD_DIM ** -0.5
    D2 = 2 * HEAD_DIM

    for hh in range(2):
        k = k_ref[:, hh * HEAD_DIM:(hh + 1) * HEAD_DIM]
        if rope:
            k = _rope(k, c_ref[...], sa_ref[...], sb_ref[...])
        kr_ref[:, hh * HEAD_DIM:(hh + 1) * HEAD_DIM] = k.astype(BF16)
    vr_ref[...] = v_ref[...].astype(BF16)

    lam = lam_ref[...]
    lam_val = (jnp.exp(jnp.sum(lam[0:1] * lam[1:2], -1, keepdims=True))
               - jnp.exp(jnp.sum(lam[2:3] * lam[3:4], -1, keepdims=True)) + lam_init)

    def block(i, carry):
        r0 = pl.multiple_of(i * tq, tq)
        vb = vr_ref[...]
        outs = []
        for hh in range(2):
            cs = slice(hh * HEAD_DIM, (hh + 1) * HEAD_DIM)
            q = q_ref[pl.ds(r0, tq), cs]
            if rope:
                q = _rope(q, c_ref[pl.ds(r0, tq), :], sa_ref[pl.ds(r0, tq), :], sb_ref[pl.ds(r0, tq), :])
            qb = q.astype(BF16)
            s = _dot_nt(qb, kr_ref[:, cs]) * scale
            m = jnp.max(s, -1, keepdims=True)
            if ctx:
                sc = _dot_nt(qb, ck_ref[:, cs].astype(BF16)) * scale
                m = jnp.maximum(m, jnp.max(sc, -1, keepdims=True))
            p = jnp.exp(s - m)
            den = jnp.sum(p, -1, keepdims=True)
            o = _dot(p.astype(BF16), vb)
            if ctx:
                pc = jnp.exp(sc - m)
                den = den + jnp.sum(pc, -1, keepdims=True)
                o = o + _dot(pc.astype(BF16), cv_ref[...].astype(BF16))
            outs.append(o / den)
        d = outs[0] - lam_val * outs[1]
        o_ref[pl.ds(r0, tq), :] = (_rms(d, sub_ref[...]) * (1.0 - lam_init)).astype(o_ref.dtype)
        return carry

    lax.fori_loop(0, T // tq, block, 0)


def _diff_attention(p, *, nb, T, row0, tq, lam, subln, lam_init, tabs=None, ctx=None, layer=0):
    D2 = 2 * HEAD_DIM
    q_col, k_col, v_col = 5120, 6144, 7168
    in_specs = [
        pl.BlockSpec((T, D2), lambda b, h: (row0 + b, q_col // D2 + h)),
        pl.BlockSpec((T, D2), lambda b, h: (row0 + b, k_col // D2 + h)),
        pl.BlockSpec((T, D2), lambda b, h: (row0 + b, v_col // D2 + h)),
    ]
    args = [p, p, p]
    if ctx is not None:
        for a in ctx:
            args.append(a.reshape(a.shape[0], a.shape[1], PAST_LEN, H_D * D2))
            in_specs.append(pl.BlockSpec((None, None, PAST_LEN, D2), lambda b, h: (b, layer, 0, h)))
    if tabs is not None:
        for a in tabs:
            args.append(a)
            in_specs.append(pl.BlockSpec((T, HEAD_DIM), lambda b, h: (0, 0)))
    args += [lam, subln.reshape(1, D2)]
    in_specs += [pl.BlockSpec((4, HEAD_DIM), lambda b, h: (0, 0)),
                 pl.BlockSpec((1, D2), lambda b, h: (0, 0))]
    kern = functools.partial(_diff_kernel, T=T, tq=tq, rope=tabs is not None,
                             ctx=ctx is not None, lam_init=lam_init)
    return pl.pallas_call(
        kern,
        grid=(nb, H_D),
        in_specs=in_specs,
        out_specs=pl.BlockSpec((T, D2), lambda b, h: (b, h)),
        out_shape=jax.ShapeDtypeStruct((nb * T, H_D * D2), BF16),
        scratch_shapes=[pltpu.VMEM((T, D2), BF16), pltpu.VMEM((T, D2), BF16)],
        compiler_params=_params(40, 2),
        name="diff_attn",
    )(*args)


def _head_norm(x):
    mu = jnp.mean(x, -1, keepdims=True)
    xc = x - mu
    var = jnp.mean(xc * xc, -1, keepdims=True)
    return xc * lax.rsqrt(var + NORM_EPS)


def _silu(x):
    return x * jax.nn.sigmoid(x)


def _ret_kernel(*refs, T, rope, has_state):
    it = iter(refs)
    q_ref, k_ref, v_ref, gf_ref, gb_ref, dec_ref = (next(it) for _ in range(6))
    s0f_ref = s0b_ref = c_ref = sa_ref = sb_ref = None
    if has_state:
        s0f_ref, s0b_ref = next(it), next(it)
    if rope:
        c_ref, sa_ref, sb_ref = next(it), next(it), next(it)
    o_ref, sf_ref, sb_out_ref, acc_ref = next(it), next(it), next(it), next(it)
    C = RET_CHUNK
    n = T // C
    head = pl.program_id(1)
    ri = lax.broadcasted_iota(jnp.int32, (C, C), 0).astype(F32)
    ci = lax.broadcasted_iota(jnp.int32, (C, C), 1).astype(F32)

    def log_gamma(d):
        x = jnp.full((1, C), dec_ref[d, head], F32)
        return jnp.minimum(x, 0.0) - jnp.log1p(jnp.exp(-jnp.abs(x)))

    def chunk(c0):
        rows = pl.ds(c0, C)
        q = q_ref[rows, :]
        k = k_ref[rows, :] * (HEAD_DIM ** -0.5)
        if rope:
            tab = (c_ref[rows, :], sa_ref[rows, :], sb_ref[rows, :])
            q = _rope(q, *tab)
            k = _rope(k, *tab)
        return q, k, v_ref[rows, :].astype(BF16)

    def step(q, k, vb, s, dmat, xi, zeta, g_chunk):
        att = _dot_nt(q.astype(BF16), k.astype(BF16)) * dmat
        o = _dot(att.astype(BF16), vb) + _dot((q * xi).astype(BF16), s.astype(BF16))
        s = g_chunk * s + _dot((k * zeta).T.astype(BF16), vb)
        return o, s

    lg = log_gamma(0)
    diff = ri - ci
    dmat = jnp.where(diff >= 0, jnp.exp(jnp.maximum(diff, 0.0) * lg), 0.0)
    xi = jnp.exp((ri + 1.0) * lg)
    zeta = jnp.exp((C - 1.0 - ri) * lg)
    g_chunk = jnp.exp(C * lg)
    s = s0f_ref[...] if has_state else jnp.zeros((C, C), F32)
    for c in range(n):
        q, k, vb = chunk(c * C)
        o, s = step(q, k, vb, s, dmat, xi, zeta, g_chunk)
        acc_ref[pl.ds(c * C, C), :] = _silu(gf_ref[pl.ds(c * C, C), :]) * _head_norm(o)
    sf_ref[...] = s

    lg = log_gamma(1)
    diff = ci - ri
    dmat = jnp.where(diff >= 0, jnp.exp(jnp.maximum(diff, 0.0) * lg), 0.0)
    xi = jnp.exp((C - ri) * lg)
    zeta = jnp.exp(ri * lg)
    g_chunk = jnp.exp(C * lg)
    s = s0b_ref[...] if has_state else jnp.zeros((C, C), F32)
    for c in reversed(range(n)):
        q, k, vb = chunk(c * C)
        o, s = step(q, k, vb, s, dmat, xi, zeta, g_chunk)
        y = acc_ref[pl.ds(c * C, C), :] + _silu(gb_ref[pl.ds(c * C, C), :]) * _head_norm(o)
        o_ref[pl.ds(c * C, C), :] = y.astype(o_ref.dtype)
    sb_out_ref[...] = s


def _retention(p, *, nb, T, row0, decay, tabs=None, state=None, layer=0):
    hd = HEAD_DIM
    col = lambda c0: (lambda b, h: (row0 + b, c0 // hd + h))
    in_specs = [pl.BlockSpec((T, hd), col(c0)) for c0 in (0, 1024, 2048, 3072, 4096)]
    args = [p] * 5 + [decay]
    in_specs.append(pl.BlockSpec(memory_space=pltpu.SMEM))
    if state is not None:
        for a in state:
            args.append(a)
            in_specs.append(pl.BlockSpec((None, None, None, hd, hd), lambda b, h: (b, layer, h, 0, 0)))
    if tabs is not None:
        for a in tabs:
            args.append(a)
            in_specs.append(pl.BlockSpec((T, hd), lambda b, h: (0, 0)))
    kern = functools.partial(_ret_kernel, T=T, rope=tabs is not None, has_state=state is not None)
    st_spec = pl.BlockSpec((None, None, hd, hd), lambda b, h: (b, h, 0, 0))
    return pl.pallas_call(
        kern,
        grid=(nb, H_C),
        in_specs=in_specs,
        out_specs=[pl.BlockSpec((T, hd), lambda b, h: (b, h)), st_spec, st_spec],
        out_shape=[jax.ShapeDtypeStruct((nb * T, H_C * hd), BF16),
                   jax.ShapeDtypeStruct((nb, H_C, hd, hd), F32),
                   jax.ShapeDtypeStruct((nb, H_C, hd, hd), F32)],
        scratch_shapes=[pltpu.VMEM((T, hd), F32)],
        compiler_params=_params(40, 2),
        name="retention",
    )(*args)


def _route(logits):
    top_v, top_e = lax.top_k(logits, TOP_K)
    gates = jax.nn.softmax(top_v, axis=-1)
    chosen = jnp.sum((top_e[:, :, None] == jnp.arange(N_EXPERTS)[None, None, :]).astype(jnp.int32), 1)
    rank_all = jnp.cumsum(chosen, 0) - chosen
    counts = jnp.sum(chosen, 0)
    padded = (counts + MOE_TM - 1) // MOE_TM * MOE_TM
    ends_p = jnp.cumsum(padded)
    start_p = ends_p - padded
    dest = start_p[top_e] + jnp.take_along_axis(rank_all, top_e, 1)
    tok = jnp.repeat(jnp.arange(N_TOK, dtype=jnp.int32), TOP_K)
    row_tok = jnp.zeros((MOE_ROWS,), jnp.int32).at[dest.reshape(-1)].set(tok)
    block_start = jnp.arange(MOE_BLOCKS, dtype=jnp.int32) * MOE_TM
    block_e = jnp.minimum(jnp.sum((ends_p[None, :] <= block_start[:, None]).astype(jnp.int32), 1),
                          N_EXPERTS - 1).astype(jnp.int32)
    n_used = (ends_p[-1] // MOE_TM).astype(jnp.int32).reshape(1)
    return dest.astype(jnp.int32), gates, row_tok, block_e, n_used


GATHER_R = 256


def _gather_kernel(tok_ref, h_hbm, o_ref, sem):
    def copy(r):
        src = pl.multiple_of(tok_ref[0, 0, r] * CHUNKS, CHUNKS)
        return pltpu.make_async_copy(h_hbm.at[pl.ds(src, CHUNKS), :],
                                     o_ref.at[pl.ds(r * CHUNKS, CHUNKS), :], sem)

    def start(r, c):
        copy(r).start()
        return c

    lax.fori_loop(0, GATHER_R, start, 0, unroll=8)
    for r in range(GATHER_R):
        copy(r).wait()


def _gather_rows(h, row_tok):
    steps = MOE_ROWS // GATHER_R
    return pl.pallas_call(
        _gather_kernel,
        grid=(steps,),
        in_specs=[
            pl.BlockSpec((1, 1, GATHER_R), lambda i: (i, 0, 0), memory_space=pltpu.SMEM),
            pl.BlockSpec(memory_space=pl.ANY),
        ],
        out_specs=pl.BlockSpec((GATHER_R * CHUNKS, LANES), lambda i: (i, 0)),
        out_shape=jax.ShapeDtypeStruct((MOE_ROWS * CHUNKS, LANES), F32),
        scratch_shapes=[pltpu.SemaphoreType.DMA(())],
        compiler_params=_params(32, 1),
        name="moe_gather",
    )(row_tok.reshape(steps, 1, GATHER_R), h)


def _new_expert(be_ref, m):
    prev = be_ref[jnp.maximum(m - 1, 0)]
    return jnp.logical_or(m == 0, be_ref[m] != prev)


def _gate_up_kernel(be_ref, nu_ref, x_ref, wg_ref, bg_ref, wu_ref, bu_ref, o_ref, wgb_ref, wub_ref):
    m = pl.program_id(1)

    @pl.when(_new_expert(be_ref, m))
    def _():
        wgb_ref[...] = wg_ref[...].astype(BF16)
        wub_ref[...] = wu_ref[...].astype(BF16)

    @pl.when(m < nu_ref[0])
    def _():
        xb = jnp.concatenate([_load_chunk(x_ref, j, MOE_TM).astype(BF16) for j in range(CHUNKS)], axis=1)
        g = jnp.minimum(_dot(xb, wgb_ref[...]) + bg_ref[...], SWIGLU_LIMIT)
        u = jnp.clip(_dot(xb, wub_ref[...]) + bu_ref[...], -SWIGLU_LIMIT, SWIGLU_LIMIT)
        o_ref[...] = (g * jax.nn.sigmoid(SWIGLU_ALPHA * g) * (u + 1.0)).astype(o_ref.dtype)

    @pl.when(m >= nu_ref[0])
    def _():
        o_ref[...] = jnp.zeros_like(o_ref)


def _gate_up(layer, xs, block_e, n_used, w_gate, b_gate, w_up, b_up):
    tm, tn = MOE_TM, MOE_TN
    wspec = pl.BlockSpec((None, None, D_MODEL, tn), lambda j, i, be, nu: (layer, be[i], 0, j))
    bspec = pl.BlockSpec((None, None, 1, tn), lambda j, i, be, nu: (layer, be[i], 0, j))
    return pl.pallas_call(
        _gate_up_kernel,
        grid_spec=pltpu.PrefetchScalarGridSpec(
            num_scalar_prefetch=2,
            grid=(D_FF // tn, MOE_BLOCKS),
            in_specs=[pl.BlockSpec((tm * CHUNKS, LANES), lambda j, i, be, nu: (i, 0)),
                      wspec, bspec, wspec, bspec],
            out_specs=pl.BlockSpec((tm, tn), lambda j, i, be, nu: (i, j)),
            scratch_shapes=[pltpu.VMEM((D_MODEL, tn), BF16), pltpu.VMEM((D_MODEL, tn), BF16)],
        ),
        out_shape=jax.ShapeDtypeStruct((MOE_ROWS, D_FF), BF16),
        compiler_params=_params(56, 2),
        name="moe_gate_up",
    )(block_e, n_used, xs, w_gate, b_gate.reshape(DEPTH, N_EXPERTS, 1, D_FF),
      w_up, b_up.reshape(DEPTH, N_EXPERTS, 1, D_FF))


def _down_kernel(be_ref, nu_ref, h_ref, wd_ref, bd_ref, o_ref, wdb_ref):
    m = pl.program_id(0)

    @pl.when(_new_expert(be_ref, m))
    def _():
        wdb_ref[...] = wd_ref[...].astype(BF16)

    @pl.when(m < nu_ref[0])
    def _():
        _store_chunked(o_ref, _dot(h_ref[...], wdb_ref[...]) + bd_ref[...], MOE_TM)

    @pl.when(m >= nu_ref[0])
    def _():
        o_ref[...] = jnp.zeros_like(o_ref)


def _down(layer, hs, block_e, n_used, w_down, b_down):
    tm = MOE_TM
    return pl.pallas_call(
        _down_kernel,
        grid_spec=pltpu.PrefetchScalarGridSpec(
            num_scalar_prefetch=2,
            grid=(MOE_BLOCKS,),
            in_specs=[pl.BlockSpec((tm, D_FF), lambda i, be, nu: (i, 0)),
                      pl.BlockSpec((None, None, D_FF, D_MODEL), lambda i, be, nu: (layer, be[i], 0, 0)),
                      pl.BlockSpec((None, None, 1, D_MODEL), lambda i, be, nu: (layer, be[i], 0, 0))],
            out_specs=pl.BlockSpec((tm * CHUNKS, LANES), lambda i, be, nu: (i, 0)),
            scratch_shapes=[pltpu.VMEM((D_FF, D_MODEL), BF16)],
        ),
        out_shape=jax.ShapeDtypeStruct((MOE_ROWS * CHUNKS, LANES), F32),
        compiler_params=_params(56, 1),
        name="moe_down",
    )(block_e, n_used, hs, w_down, b_down.reshape(DEPTH, N_EXPERTS, 1, D_MODEL))


COMBINE_T = 128


def _combine_kernel(dest_ref, ys_hbm, gates_ref, x_ref, gate_ref, lng_ref, lnb_ref, xo_ref,
                    b0_ref, b1_ref, b2_ref, b3_ref, sem):
    bufs = (b0_ref, b1_ref, b2_ref, b3_ref)

    def copy(t, k):
        src = pl.multiple_of(dest_ref[0, 0, t * TOP_K + k] * CHUNKS, CHUNKS)
        return pltpu.make_async_copy(ys_hbm.at[pl.ds(src, CHUNKS), :],
                                     bufs[k].at[pl.ds(t * CHUNKS, CHUNKS), :], sem)

    def start(t, c):
        for k in range(TOP_K):
            copy(t, k).start()
        return c

    lax.fori_loop(0, COMBINE_T, start, 0, unroll=4)
    for t in range(COMBINE_T):
        for k in range(TOP_K):
            copy(t, k).wait()
    gates = gates_ref[...]
    gk = [jnp.broadcast_to(gates[:, k:k + 1], (COMBINE_T, LANES)) for k in range(TOP_K)]
    parts = []
    for j in range(CHUNKS):
        rows = [gk[k] * _load_chunk(bufs[k], j, COMBINE_T) for k in range(TOP_K)]
        parts.append((rows[0] + rows[1]) + (rows[2] + rows[3]))
    f = jnp.concatenate(parts, axis=1)
    z = DEEPNORM_ALPHA * x_ref[...] + gate_ref[0] * f
    xo_ref[...] = _layer_norm(z, lng_ref[...], lnb_ref[...])


def _combine(ys, dest, gates, x, gate, ln_g, ln_b):
    t = COMBINE_T
    steps = N_TOK // t
    d = D_MODEL
    return pl.pallas_call(
        _combine_kernel,
        grid=(steps,),
        in_specs=[
            pl.BlockSpec((1, 1, t * TOP_K), lambda i: (i, 0, 0), memory_space=pltpu.SMEM),
            pl.BlockSpec(memory_space=pl.ANY),
            pl.BlockSpec((t, TOP_K), lambda i: (i, 0)),
            pl.BlockSpec((t, d), lambda i: (i, 0)),
            pl.BlockSpec((1, 1, d), lambda i: (i * t // SEG, 0, 0)),
            pl.BlockSpec((1, d), lambda i: (0, 0)),
            pl.BlockSpec((1, d), lambda i: (0, 0)),
        ],
        out_specs=pl.BlockSpec((t, d), lambda i: (i, 0)),
        out_shape=jax.ShapeDtypeStruct((N_TOK, d), F32),
        scratch_shapes=[pltpu.VMEM((t * CHUNKS, LANES), F32)] * TOP_K + [pltpu.SemaphoreType.DMA(())],
        compiler_params=_params(32, 1),
        name="moe_combine",
    )(dest.reshape(steps, 1, t * TOP_K), ys, gates, x, gate, ln_g.reshape(1, d), ln_b.reshape(1, d))


def _moe(layer, h, logits, x, gate, ln_g, ln_b, w_gate, b_gate, w_up, b_up, w_down, b_down):
    dest, gates, row_tok, block_e, n_used = _route(logits)
    xs = _gather_rows(h, row_tok)
    hs = _gate_up(layer, xs, block_e, n_used, w_gate, b_gate, w_up, b_up)
    ys = _down(layer, hs, block_e, n_used, w_down, b_down)
    return _combine(ys, dest, gates, x, gate, ln_g, ln_b)


def _even_mixer(p, i, tabs, sink, qg, kg, cache_k_a, cache_v_a, cache_k_b, cache_v_b):
    pb, sb = N_PROMPT // SEQ, N_PROMPT // DEC_SEQ
    norm = (qg, kg)
    oa_p = _attention(p, nb=BATCH, T=SEQ, row0=0, q_col=0, k_col=1024, v_col=1280,
                      n_kv=KV_A, G=H_A // KV_A, tq=SEQ, sink=sink)
    ob_p, kb_n = _attention(p, nb=BATCH, T=SEQ, row0=0, q_col=1536, k_col=2560, v_col=2816,
                            n_kv=KV_B, G=H_B // KV_B, tq=SEQ, norm=norm, emit_kn=True)
    oa_s = _attention(p, nb=DEC_BATCH, T=DEC_SEQ, row0=sb, q_col=0, k_col=1024, v_col=1280,
                      n_kv=KV_A, G=H_A // KV_A, tq=WINDOW, tabs=tabs, ctx=(cache_k_a, cache_v_a),
                      layer=i, window=True, sink=sink)
    ob_s = _attention(p, nb=DEC_BATCH, T=DEC_SEQ, row0=sb, q_col=1536, k_col=2560, v_col=2816,
                      n_kv=KV_B, G=H_B // KV_B, tq=256, tabs=tabs, ctx=(cache_k_b, cache_v_b),
                      layer=i, norm=norm)
    del pb
    pp = p[:N_PROMPT]
    ka = pp[:, 1024:1280].reshape(BATCH, SEQ, KV_A, HEAD_DIM)
    va = pp[:, 1280:1536].reshape(BATCH, SEQ, KV_A, HEAD_DIM)
    kb = kb_n.reshape(BATCH, SEQ, KV_B, HEAD_DIM)
    vb = pp[:, 2816:3072].reshape(BATCH, SEQ, KV_B, HEAD_DIM)
    return (jnp.concatenate([oa_p, oa_s], 0), jnp.concatenate([ob_p, ob_s], 0), (ka, va, kb, vb))


def _odd_mixer(p, i, layer, tabs, decay, lam, subln, state_f, state_b, cache_k_d, cache_v_d):
    sb = N_PROMPT // DEC_SEQ
    lam_init = 0.8 - 0.6 * math.exp(-0.3 * layer)
    yc_p, sf, sbk = _retention(p, nb=BATCH, T=SEQ, row0=0, decay=decay)
    yc_s, _, _ = _retention(p, nb=DEC_BATCH, T=DEC_SEQ, row0=sb, decay=decay, tabs=tabs,
                            state=(state_f, state_b), layer=i)
    yd_p = _diff_attention(p, nb=BATCH, T=SEQ, row0=0, tq=SEQ, lam=lam, subln=subln, lam_init=lam_init)
    yd_s = _diff_attention(p, nb=DEC_BATCH, T=DEC_SEQ, row0=sb, tq=256, lam=lam, subln=subln,
                           lam_init=lam_init, tabs=tabs, ctx=(cache_k_d, cache_v_d), layer=i)
    pp = p[:N_PROMPT]
    kd = pp[:, 6144:7168].reshape(BATCH, SEQ, H_D, 2 * HEAD_DIM)
    vd = pp[:, 7168:8192].reshape(BATCH, SEQ, H_D, 2 * HEAD_DIM)
    return (jnp.concatenate([yc_p, yc_s], 0), jnp.concatenate([yd_p, yd_s], 0), (sf, sbk, kd, vd))


def kernel(x_prompt, x_sample, c, cache_k_a, cache_v_a, cache_k_b, cache_v_b, state_ret_fwd, state_ret_bwd, cache_k_d, cache_v_d, c_ctx, w_ada, b_ada, ln_g, ln_b, w_in_ab, w_out_ab, sink_a, qnorm_b, knorm_b, w_in_cd, w_out_cd, ret_decay, lam_d, subln_d, w_router, b_router, w_gate, b_gate, w_up, b_up, w_down, b_down):
    x = jnp.concatenate([x_prompt.reshape(N_PROMPT, D_MODEL), x_sample.reshape(N_SAMPLE, D_MODEL)], 0)
    cond = jnp.concatenate([c_ctx[None], c, jnp.zeros((16 - 1 - DEC_BATCH, D_MODEL), F32)], 0)
    mods = _ada(cond, w_ada, b_ada)
    seg_row = jnp.array([0] * (N_PROMPT // SEG) + list(range(1, DEC_BATCH + 1)), jnp.int32)
    mods = mods[:, seg_row].reshape(DEPTH, N_SEG, 6, 1, D_MODEL)
    tabs = _rope_tables()
    caches = ([], [], [], [], [], [], [], [])
    for l in range(DEPTH):
        i = l // 2
        mod = [mods[l, :, j] for j in range(6)]
        if l % 2 == 0:
            p = _proj(x, mod[1], mod[0], w_in_ab, i)
            a1, a2, new = _even_mixer(p, i, tabs, sink_a[i], qnorm_b[i], knorm_b[i],
                                      cache_k_a, cache_v_a, cache_k_b, cache_v_b)
            w_out = w_out_ab[i]
            for dst, val in zip(caches[:4], new):
                dst.append(val)
        else:
            p = _proj(x, mod[1], mod[0], w_in_cd, i)
            a1, a2, new = _odd_mixer(p, i, l, tabs, ret_decay[i], lam_d[i], subln_d[i],
                                     state_ret_fwd, state_ret_bwd, cache_k_d, cache_v_d)
            w_out = w_out_cd[i]
            for dst, val in zip(caches[4:], new):
                dst.append(val)
        x, h, logits = _oproj(a1, a2, w_out, x, mod[2], mod[4], mod[3], ln_g[l, 0], ln_b[l, 0],
                              w_router[l], b_router[l])
        x = _moe(l, h, logits, x, mod[5], ln_g[l, 1], ln_b[l, 1],
                 w_gate, b_gate, w_up, b_up, w_down, b_down)
    xp = x[:N_PROMPT].reshape(BATCH, SEQ, D_MODEL)
    xs = x[N_PROMPT:].reshape(DEC_BATCH, DEC_SEQ, D_MODEL)
    return (xp, xs) + tuple(jnp.stack(v, 1) for v in caches)
```

```python
import functools
import math

import jax
import jax.numpy as jnp
from jax import lax
from jax.experimental import pallas as pl
from jax.experimental.pallas import tpu as pltpu

F32 = jnp.float32
BF16 = jnp.bfloat16

D_MODEL = 2048
BATCH = 16
SEQ = 256
DEPTH = 4
DEC_BATCH = 8
DEC_SEQ = 1024
PAST_LEN = 256
GRID_W = 64
HEAD_DIM = 128
H_A = 8
KV_A = 2
WINDOW = 128
H_B = 8
KV_B = 2
H_C = 8
H_D = 4
RET_CHUNK = 128
ROPE_THETA = 10000.0
N_EXPERTS = 32
TOP_K = 4
D_FF = 2048
SWIGLU_LIMIT = 7.0
SWIGLU_ALPHA = 1.702
DEEPNORM_ALPHA = (2 * DEPTH) ** 0.25
NORM_EPS = 1e-6

N_PROMPT = BATCH * SEQ
N_SAMPLE = DEC_BATCH * DEC_SEQ
N_TOK = N_PROMPT + N_SAMPLE
SEG = DEC_SEQ
N_SEG = N_TOK // SEG
EVEN_IN = 3072
ODD_IN = 8192

MOE_TM = 256
MOE_TN = 1024
LANES = 128
CHUNKS = D_MODEL // LANES
N_ASSIGN = N_TOK * TOP_K
MOE_BLOCKS = (N_ASSIGN + N_EXPERTS * (MOE_TM - 1) + MOE_TM - 1) // MOE_TM
MOE_ROWS = MOE_BLOCKS * MOE_TM

MIB = 1024 * 1024


def _params(vmem_mib, n_axes):
    return pltpu.CompilerParams(
        dimension_semantics=("arbitrary",) * n_axes,
        vmem_limit_bytes=vmem_mib * MIB)


def _dot(a, b):
    return jnp.dot(a, b, preferred_element_type=F32)


def _dot_nt(a, b):
    return lax.dot_general(a, b, (((1,), (1,)), ((), ())), preferred_element_type=F32)


def _layer_norm(z, g, b):
    mu = jnp.mean(z, -1, keepdims=True)
    zc = z - mu
    var = jnp.mean(zc * zc, -1, keepdims=True)
    return zc * lax.rsqrt(var + NORM_EPS) * g + b


def _ada_kernel(c_ref, w_ref, b_ref, o_ref):
    c = c_ref[...]
    a = c * jax.nn.sigmoid(c)
    o_ref[0] = _dot(a.astype(BF16), w_ref[0].astype(BF16)) + b_ref[0]


def _ada(cond, w_ada, b_ada):
    tn = 1024
    n = w_ada.shape[-1]
    return pl.pallas_call(
        _ada_kernel,
        grid=(DEPTH, n // tn),
        in_specs=[
            pl.BlockSpec((16, D_MODEL), lambda l, j: (0, 0)),
            pl.BlockSpec((1, D_MODEL, tn), lambda l, j: (l, 0, j)),
            pl.BlockSpec((1, 1, tn), lambda l, j: (l, 0, j)),
        ],
        out_specs=pl.BlockSpec((1, 16, tn), lambda l, j: (l, 0, j)),
        out_shape=jax.ShapeDtypeStruct((DEPTH, 16, n), F32),
        compiler_params=_params(40, 2),
        name="ada",
    )(cond, w_ada, b_ada.reshape(DEPTH, 1, n))


def _proj_kernel(x_ref, sc_ref, sh_ref, w_ref, o_ref, wb_ref):
    @pl.when(pl.program_id(1) == 0)
    def _():
        wb_ref[...] = w_ref[...].astype(BF16)

    h = x_ref[...] * (1.0 + sc_ref[0]) + sh_ref[0]
    o_ref[...] = _dot(h.astype(BF16), wb_ref[...])


def _proj(x, scale, shift, w, layer):
    tm, tn = 512, 1024
    m, k = x.shape
    n = w.shape[2]
    return pl.pallas_call(
        _proj_kernel,
        grid=(n // tn, m // tm),
        in_specs=[
            pl.BlockSpec((tm, k), lambda j, i: (i, 0)),
            pl.BlockSpec((1, 1, k), lambda j, i: (i * tm // SEG, 0, 0)),
            pl.BlockSpec((1, 1, k), lambda j, i: (i * tm // SEG, 0, 0)),
            pl.BlockSpec((None, k, tn), lambda j, i: (layer, 0, j)),
        ],
        out_specs=pl.BlockSpec((tm, tn), lambda j, i: (i, j)),
        out_shape=jax.ShapeDtypeStruct((m, n), F32),
        scratch_shapes=[pltpu.VMEM((k, tn), BF16)],
        compiler_params=_params(48, 2),
        name="proj",
    )(x, scale, shift, w)


def _store_chunked(ref, val, n):
    for j in range(CHUNKS):
        ref[pl.ds(j, n, stride=CHUNKS), :] = val[:, j * LANES:(j + 1) * LANES]


def _load_chunk(ref, j, n):
    return ref[pl.ds(j, n, stride=CHUNKS), :]


def _oproj_kernel(a1_ref, a2_ref, w1_ref, w2_ref, x_ref, gate_ref, sc_ref, sh_ref,
                  lng_ref, lnb_ref, wr_ref, br_ref, xo_ref, h_ref, lg_ref):
    y = _dot(a1_ref[...], w1_ref[...]) + _dot(a2_ref[...], w2_ref[...])
    z = DEEPNORM_ALPHA * x_ref[...] + gate_ref[0] * y
    xn = _layer_norm(z, lng_ref[...], lnb_ref[...])
    xo_ref[...] = xn
    h = xn * (1.0 + sc_ref[0]) + sh_ref[0]
    _store_chunked(h_ref, h, h.shape[0])
    lg_ref[...] = _dot(h.astype(BF16), wr_ref[...]) + br_ref[...]


def _oproj(a1, a2, w_out, x, gate, scale, shift, ln_g, ln_b, w_router, b_router):
    tm = 256
    m, d = x.shape
    half = a1.shape[1]
    wb = w_out.astype(BF16)
    seg = lambda i: (i * tm // SEG, 0, 0)
    row = lambda i: (i, 0)
    fix = lambda i: (0, 0)
    return pl.pallas_call(
        _oproj_kernel,
        grid=(m // tm,),
        in_specs=[
            pl.BlockSpec((tm, half), row),
            pl.BlockSpec((tm, half), row),
            pl.BlockSpec((half, d), fix),
            pl.BlockSpec((half, d), lambda i: (1, 0)),
            pl.BlockSpec((tm, d), row),
            pl.BlockSpec((1, 1, d), seg),
            pl.BlockSpec((1, 1, d), seg),
            pl.BlockSpec((1, 1, d), seg),
            pl.BlockSpec((1, d), fix),
            pl.BlockSpec((1, d), fix),
            pl.BlockSpec((d, N_EXPERTS), fix),
            pl.BlockSpec((1, N_EXPERTS), fix),
        ],
        out_specs=[
            pl.BlockSpec((tm, d), row),
            pl.BlockSpec((tm * CHUNKS, LANES), row),
            pl.BlockSpec((tm, N_EXPERTS), row),
        ],
        out_shape=[
            jax.ShapeDtypeStruct((m, d), F32),
            jax.ShapeDtypeStruct((m * CHUNKS, LANES), F32),
            jax.ShapeDtypeStruct((m, N_EXPERTS), F32),
        ],
        compiler_params=_params(48, 1),
        name="oproj",
    )(a1, a2, wb, wb, x, gate, scale, shift, ln_g.reshape(1, d), ln_b.reshape(1, d),
      w_router.astype(BF16), b_router.reshape(1, N_EXPERTS))


def _rope_tables():
    t = jnp.arange(DEC_SEQ)
    rows = (t // GRID_W).astype(F32)
    cols = (t % GRID_W).astype(F32)
    n_freq = HEAD_DIM // 4
    inv = ROPE_THETA ** (-jnp.arange(n_freq, dtype=F32) / n_freq)
    ar = rows[:, None] * inv
    ac = cols[:, None] * inv
    zero = jnp.zeros_like(ar)
    c = jnp.concatenate([jnp.cos(ar), jnp.cos(ar), jnp.cos(ac), jnp.cos(ac)], -1)
    sa = jnp.concatenate([-jnp.sin(ar), zero, -jnp.sin(ac), zero], -1)
    sb = jnp.concatenate([zero, jnp.sin(ar), zero, jnp.sin(ac)], -1)
    return c, sa, sb


def _rope(x, c, sa, sb):
    return x * c + pltpu.roll(x, 96, 1) * sa + pltpu.roll(x, 32, 1) * sb


def _rms(x, g):
    return x * lax.rsqrt(jnp.mean(x * x, -1, keepdims=True) + NORM_EPS) * g


def _attn_kernel(*refs, T, G, tq, rope, ctx, window, sink, norm, emit_kn):
    it = iter(refs)
    q_ref, k_ref, v_ref = next(it), next(it), next(it)
    ck_ref = cv_ref = c_ref = sa_ref = sb_ref = sink_ref = qg_ref = kg_ref = kn_ref = None
    if ctx:
        ck_ref, cv_ref = next(it), next(it)
    if rope:
        c_ref, sa_ref, sb_ref = next(it), next(it), next(it)
    if sink:
        sink_ref = next(it)
    if norm:
        qg_ref, kg_ref = next(it), next(it)
    o_ref = next(it)
    if emit_kn:
        kn_ref = next(it)
    kr_ref, vr_ref = next(it), next(it)
    scale = HEAD_DIM ** -0.5
    kvh = pl.program_id(1)

    k = k_ref[...]
    if norm:
        k = _rms(k, kg_ref[...])
        if emit_kn:
            kn_ref[...] = k
    if rope:
        k = _rope(k, c_ref[...], sa_ref[...], sb_ref[...])
    kr_ref[...] = k.astype(BF16)
    vr_ref[...] = v_ref[...].astype(BF16)

    def block(i, carry):
        r0 = pl.multiple_of(i * tq, tq)
        if window:
            k0 = pl.multiple_of(jnp.clip(r0 - WINDOW, 0, T - 3 * WINDOW), WINDOW)
            nk = 3 * WINDOW
            kb = kr_ref[pl.ds(k0, nk), :]
            vb = vr_ref[pl.ds(k0, nk), :]
            qpos = r0 + lax.broadcasted_iota(jnp.int32, (tq, nk), 0)
            kpos = k0 + lax.broadcasted_iota(jnp.int32, (tq, nk), 1)
            keep = jnp.abs(qpos - kpos) <= WINDOW
        else:
            kb = kr_ref[...]
            vb = vr_ref[...]
        for h in range(G):
            q = q_ref[pl.ds(r0, tq), h * HEAD_DIM:(h + 1) * HEAD_DIM]
            if norm:
                q = _rms(q, qg_ref[...])
            if rope:
                q = _rope(q, c_ref[pl.ds(r0, tq), :], sa_ref[pl.ds(r0, tq), :], sb_ref[pl.ds(r0, tq), :])
            qb = q.astype(BF16)
            s = _dot_nt(qb, kb) * scale
            if window:
                s = jnp.where(keep, s, -1e30)
            m = jnp.max(s, -1, keepdims=True)
            if ctx:
                sc = _dot_nt(qb, ck_ref[...].astype(BF16)) * scale
                m = jnp.maximum(m, jnp.max(sc, -1, keepdims=True))
            if sink:
                sk = sink_ref[kvh * G + h]
                m = jnp.maximum(m, sk)
            p = jnp.exp(s - m)
            den = jnp.sum(p, -1, keepdims=True)
            o = _dot(p.astype(BF16), vb)
            if ctx:
                pc = jnp.exp(sc - m)
                den = den + jnp.sum(pc, -1, keepdims=True)
                o = o + _dot(pc.astype(BF16), cv_ref[...].astype(BF16))
            if sink:
                den = den + jnp.exp(sk - m)
            o_ref[pl.ds(r0, tq), h * HEAD_DIM:(h + 1) * HEAD_DIM] = (o / den).astype(o_ref.dtype)
        return carry

    lax.fori_loop(0, T // tq, block, 0)


def _attention(p, *, nb, T, row0, q_col, k_col, v_col, n_kv, G, tq, tabs=None, ctx=None,
               layer=0, window=False, sink=None, norm=None, emit_kn=False):
    gw = G * HEAD_DIM
    in_specs = [
        pl.BlockSpec((T, gw), lambda b, k: (row0 + b, q_col // gw + k)),
        pl.BlockSpec((T, HEAD_DIM), lambda b, k: (row0 + b, k_col // HEAD_DIM + k)),
        pl.BlockSpec((T, HEAD_DIM), lambda b, k: (row0 + b, v_col // HEAD_DIM + k)),
    ]
    args = [p, p, p]
    if ctx is not None:
        for a in ctx:
            args.append(a.reshape(a.shape[0], a.shape[1], PAST_LEN, n_kv * HEAD_DIM))
            in_specs.append(pl.BlockSpec((None, None, PAST_LEN, HEAD_DIM), lambda b, k: (b, layer, 0, k)))
    if tabs is not None:
        for a in tabs:
            args.append(a)
            in_specs.append(pl.BlockSpec((T, HEAD_DIM), lambda b, k: (0, 0)))
    if sink is not None:
        args.append(sink)
        in_specs.append(pl.BlockSpec(memory_space=pltpu.SMEM))
    if norm is not None:
        for a in norm:
            args.append(a.reshape(1, HEAD_DIM))
            in_specs.append(pl.BlockSpec((1, HEAD_DIM), lambda b, k: (0, 0)))
    out_specs = [pl.BlockSpec((T, gw), lambda b, k: (b, k))]
    out_shape = [jax.ShapeDtypeStruct((nb * T, n_kv * gw), BF16)]
    if emit_kn:
        out_specs.append(pl.BlockSpec((T, HEAD_DIM), lambda b, k: (b, k)))
        out_shape.append(jax.ShapeDtypeStruct((nb * T, n_kv * HEAD_DIM), F32))
    kern = functools.partial(
        _attn_kernel, T=T, G=G, tq=tq, rope=tabs is not None, ctx=ctx is not None,
        window=window, sink=sink is not None, norm=norm is not None, emit_kn=emit_kn)
    out = pl.pallas_call(
        kern,
        grid=(nb, n_kv),
        in_specs=in_specs,
        out_specs=out_specs,
        out_shape=out_shape,
        scratch_shapes=[pltpu.VMEM((T, HEAD_DIM), BF16), pltpu.VMEM((T, HEAD_DIM), BF16)],
        compiler_params=_params(40, 2),
        name="attn",
    )(*args)
    return out if emit_kn else out[0]


def _diff_kernel(*refs, T, tq, rope, ctx, lam_init):
    it = iter(refs)
    q_ref, k_ref, v_ref = next(it), next(it), next(it)
    ck_ref = cv_ref = c_ref = sa_ref = sb_ref = None
    if ctx:
        ck_ref, cv_ref = next(it), next(it)
    if rope:
        c_ref, sa_ref, sb_ref = next(it), next(it), next(it)
    lam_ref, sub_ref, o_ref, kr_ref, vr_ref = next(it), next(it), next(it), next(it), next(it)
    scale = HEAD_DIM ** -0.5
    D2 = 2 * HEAD_DIM

    for hh in range(2):
        k = k_ref[:, hh * HEAD_DIM:(hh + 1) * HEAD_DIM]
        if rope:
            k = _rope(k, c_ref[...], sa_ref[...], sb_ref[...])
        kr_ref[:, hh * HEAD_DIM:(hh + 1) * HEAD_DIM] = k.astype(BF16)
    vr_ref[...] = v_ref[...].astype(BF16)

    lam = lam_ref[...]
    lam_val = (jnp.exp(jnp.sum(lam[0:1] * lam[1:2], -1, keepdims=True))
               - jnp.exp(jnp.sum(lam[2:3] * lam[3:4], -1, keepdims=True)) + lam_init)

    def block(i, carry):
        r0 = pl.multiple_of(i * tq, tq)
        vb = vr_ref[...]
        outs = []
        for hh in range(2):
            cs = slice(hh * HEAD_DIM, (hh + 1) * HEAD_DIM)
            q = q_ref[pl.ds(r0, tq), cs]
            if rope:
                q = _rope(q, c_ref[pl.ds(r0, tq), :], sa_ref[pl.ds(r0, tq), :], sb_ref[pl.ds(r0, tq), :])
            qb = q.astype(BF16)
            s = _dot_nt(qb, kr_ref[:, cs]) * scale
            m = jnp.max(s, -1, keepdims=True)
            if ctx:
                sc = _dot_nt(qb, ck_ref[:, cs].astype(BF16)) * scale
                m = jnp.maximum(m, jnp.max(sc, -1, keepdims=True))
            p = jnp.exp(s - m)
            den = jnp.sum(p, -1, keepdims=True)
            o = _dot(p.astype(BF16), vb)
            if ctx:
                pc = jnp.exp(sc - m)
                den = den + jnp.sum(pc, -1, keepdims=True)
                o = o + _dot(pc.astype(BF16), cv_ref[...].astype(BF16))
            outs.append(o / den)
        d = outs[0] - lam_val * outs[1]
        o_ref[pl.ds(r0, tq), :] = (_rms(d, sub_ref[...]) * (1.0 - lam_init)).astype(o_ref.dtype)
        return carry

    lax.fori_loop(0, T // tq, block, 0)


def _diff_attention(p, *, nb, T, row0, tq, lam, subln, lam_init, tabs=None, ctx=None, layer=0):
    D2 = 2 * HEAD_DIM
    q_col, k_col, v_col = 5120, 6144, 7168
    in_specs = [
        pl.BlockSpec((T, D2), lambda b, h: (row0 + b, q_col // D2 + h)),
        pl.BlockSpec((T, D2), lambda b, h: (row0 + b, k_col // D2 + h)),
        pl.BlockSpec((T, D2), lambda b, h: (row0 + b, v_col // D2 + h)),
    ]
    args = [p, p, p]
    if ctx is not None:
        for a in ctx:
            args.append(a.reshape(a.shape[0], a.shape[1], PAST_LEN, H_D * D2))
            in_specs.append(pl.BlockSpec((None, None, PAST_LEN, D2), lambda b, h: (b, layer, 0, h)))
    if tabs is not None:
        for a in tabs:
            args.append(a)
            in_specs.append(pl.BlockSpec((T, HEAD_DIM), lambda b, h: (0, 0)))
    args += [lam, subln.reshape(1, D2)]
    in_specs += [pl.BlockSpec((4, HEAD_DIM), lambda b, h: (0, 0)),
                 pl.BlockSpec((1, D2), lambda b, h: (0, 0))]
    kern = functools.partial(_diff_kernel, T=T, tq=tq, rope=tabs is not None,
                             ctx=ctx is not None, lam_init=lam_init)
    return pl.pallas_call(
        kern,
        grid=(nb, H_D),
        in_specs=in_specs,
        out_specs=pl.BlockSpec((T, D2), lambda b, h: (b, h)),
        out_shape=jax.ShapeDtypeStruct((nb * T, H_D * D2), BF16),
        scratch_shapes=[pltpu.VMEM((T, D2), BF16), pltpu.VMEM((T, D2), BF16)],
        compiler_params=_params(40, 2),
        name="diff_attn",
    )(*args)


def _head_norm(x):
    mu = jnp.mean(x, -1, keepdims=True)
    xc = x - mu
    var = jnp.mean(xc * xc, -1, keepdims=True)
    return xc * lax.rsqrt(var + NORM_EPS)


def _silu(x):
    return x * jax.nn.sigmoid(x)


def _ret_kernel(*refs, T, rope, has_state):
    it = iter(refs)
    q_ref, k_ref, v_ref, gf_ref, gb_ref, dec_ref = (next(it) for _ in range(6))
    s0f_ref = s0b_ref = c_ref = sa_ref = sb_ref = None
    if has_state:
        s0f_ref, s0b_ref = next(it), next(it)
    if rope:
        c_ref, sa_ref, sb_ref = next(it), next(it), next(it)
    o_ref, sf_ref, sb_out_ref, acc_ref = next(it), next(it), next(it), next(it)
    C = RET_CHUNK
    n = T // C
    head = pl.program_id(1)
    ri = lax.broadcasted_iota(jnp.int32, (C, C), 0).astype(F32)
    ci = lax.broadcasted_iota(jnp.int32, (C, C), 1).astype(F32)

    def log_gamma(d):
        x = jnp.full((1, C), dec_ref[d, head], F32)
        return jnp.minimum(x, 0.0) - jnp.log1p(jnp.exp(-jnp.abs(x)))

    def chunk(c0):
        rows = pl.ds(c0, C)
        q = q_ref[rows, :]
        k = k_ref[rows, :] * (HEAD_DIM ** -0.5)
        if rope:
            tab = (c_ref[rows, :], sa_ref[rows, :], sb_ref[rows, :])
            q = _rope(q, *tab)
            k = _rope(k, *tab)
        return q, k, v_ref[rows, :].astype(BF16)

    def step(q, k, vb, s, dmat, xi, zeta, g_chunk):
        att = _dot_nt(q.astype(BF16), k.astype(BF16)) * dmat
        o = _dot(att.astype(BF16), vb) + _dot((q * xi).astype(BF16), s.astype(BF16))
        s = g_chunk * s + _dot((k * zeta).T.astype(BF16), vb)
        return o, s

    lg = log_gamma(0)
    diff = ri - ci
    dmat = jnp.where(diff >= 0, jnp.exp(jnp.maximum(diff, 0.0) * lg), 0.0)
    xi = jnp.exp((ri + 1.0) * lg)
    zeta = jnp.exp((C - 1.0 - ri) * lg)
    g_chunk = jnp.exp(C * lg)
    s = s0f_ref[...] if has_state else jnp.zeros((C, C), F32)
    for c in range(n):
        q, k, vb = chunk(c * C)
        o, s = step(q, k, vb, s, dmat, xi, zeta, g_chunk)
        acc_ref[pl.ds(c * C, C), :] = _silu(gf_ref[pl.ds(c * C, C), :]) * _head_norm(o)
    sf_ref[...] = s

    lg = log_gamma(1)
    diff = ci - ri
    dmat = jnp.where(diff >= 0, jnp.exp(jnp.maximum(diff, 0.0) * lg), 0.0)
    xi = jnp.exp((C - ri) * lg)
    zeta = jnp.exp(ri * lg)
    g_chunk = jnp.exp(C * lg)
    s = s0b_ref[...] if has_state else jnp.zeros((C, C), F32)
    for c in reversed(range(n)):
        q, k, vb = chunk(c * C)
        o, s = step(q, k, vb, s, dmat, xi, zeta, g_chunk)
        y = acc_ref[pl.ds(c * C, C), :] + _silu(gb_ref[pl.ds(c * C, C), :]) * _head_norm(o)
        o_ref[pl.ds(c * C, C), :] = y.astype(o_ref.dtype)
    sb_out_ref[...] = s


def _retention(p, *, nb, T, row0, decay, tabs=None, state=None, layer=0):
    hd = HEAD_DIM
    col = lambda c0: (lambda b, h: (row0 + b, c0 // hd + h))
    in_specs = [pl.BlockSpec((T, hd), col(c0)) for c0 in (0, 1024, 2048, 3072, 4096)]
    args = [p] * 5 + [decay]
    in_specs.append(pl.BlockSpec(memory_space=pltpu.SMEM))
    if state is not None:
        for a in state:
            args.append(a)
            in_specs.append(pl.BlockSpec((None, None, None, hd, hd), lambda b, h: (b, layer, h, 0, 0)))
    if tabs is not None:
        for a in tabs:
            args.append(a)
            in_specs.append(pl.BlockSpec((T, hd), lambda b, h: (0, 0)))
    kern = functools.partial(_ret_kernel, T=T, rope=tabs is not None, has_state=state is not None)
    st_spec = pl.BlockSpec((None, None, hd, hd), lambda b, h: (b, h, 0, 0))
    return pl.pallas_call(
        kern,
        grid=(nb, H_C),
        in_specs=in_specs,
        out_specs=[pl.BlockSpec((T, hd), lambda b, h: (b, h)), st_spec, st_spec],
        out_shape=[jax.ShapeDtypeStruct((nb * T, H_C * hd), BF16),
                   jax.ShapeDtypeStruct((nb, H_C, hd, hd), F32),
                   jax.ShapeDtypeStruct((nb, H_C, hd, hd), F32)],
        scratch_shapes=[pltpu.VMEM((T, hd), F32)],
        compiler_params=_params(40, 2),
        name="retention",
    )(*args)


def _route(logits):
    top_v, top_e = lax.top_k(logits, TOP_K)
    gates = jax.nn.softmax(top_v, axis=-1)
    chosen = jnp.sum((top_e[:, :, None] == jnp.arange(N_EXPERTS)[None, None, :]).astype(jnp.int32), 1)
    rank_all = jnp.cumsum(chosen, 0) - chosen
    counts = jnp.sum(chosen, 0)
    padded = (counts + MOE_TM - 1) // MOE_TM * MOE_TM
    ends_p = jnp.cumsum(padded)
    start_p = ends_p - padded
    dest = start_p[top_e] + jnp.take_along_axis(rank_all, top_e, 1)
    tok = jnp.repeat(jnp.arange(N_TOK, dtype=jnp.int32), TOP_K)
    row_tok = jnp.zeros((MOE_ROWS,), jnp.int32).at[dest.reshape(-1)].set(tok)
    block_start = jnp.arange(MOE_BLOCKS, dtype=jnp.int32) * MOE_TM
    block_e = jnp.minimum(jnp.sum((ends_p[None, :] <= block_start[:, None]).astype(jnp.int32), 1),
                          N_EXPERTS - 1).astype(jnp.int32)
    n_used = (ends_p[-1] // MOE_TM).astype(jnp.int32).reshape(1)
    return dest.astype(jnp.int32), gates, row_tok, block_e, n_used


GATHER_R = 256


def _gather_kernel(tok_ref, prev_ref, h_hbm, o_hbm, sem):
    i = pl.program_id(0)
    last = pl.num_programs(0) - 1

    def copy(ref, step, r):
        src = pl.multiple_of(ref[0, 0, r] * CHUNKS, CHUNKS)
        dst = pl.multiple_of((step * GATHER_R + r) * CHUNKS, CHUNKS)
        return pltpu.make_async_copy(h_hbm.at[pl.ds(src, CHUNKS), :], o_hbm.at[pl.ds(dst, CHUNKS), :], sem)

    def start(r, c):
        copy(tok_ref, i, r).start()
        return c

    lax.fori_loop(0, GATHER_R, start, 0, unroll=8)

    @pl.when(i > 0)
    def _():
        for r in range(GATHER_R):
            copy(prev_ref, i - 1, r).wait()

    @pl.when(i == last)
    def _():
        for r in range(GATHER_R):
            copy(tok_ref, i, r).wait()


def _gather_rows(h, row_tok):
    steps = MOE_ROWS // GATHER_R
    tok = row_tok.reshape(steps, 1, GATHER_R)
    return pl.pallas_call(
        _gather_kernel,
        grid=(steps,),
        in_specs=[
            pl.BlockSpec((1, 1, GATHER_R), lambda i: (i, 0, 0), memory_space=pltpu.SMEM),
            pl.BlockSpec((1, 1, GATHER_R), lambda i: (jnp.maximum(i - 1, 0), 0, 0), memory_space=pltpu.SMEM),
            pl.BlockSpec(memory_space=pl.ANY),
        ],
        out_specs=pl.BlockSpec(memory_space=pl.ANY),
        out_shape=jax.ShapeDtypeStruct((MOE_ROWS * CHUNKS, LANES), F32),
        scratch_shapes=[pltpu.SemaphoreType.DMA(())],
        compiler_params=_params(32, 1),
        name="moe_gather",
    )(tok, tok, h)


def _new_expert(be_ref, m):
    prev = be_ref[jnp.maximum(m - 1, 0)]
    return jnp.logical_or(m == 0, be_ref[m] != prev)


def _gate_up_kernel(be_ref, nu_ref, x_ref, wg_ref, bg_ref, wu_ref, bu_ref, o_ref, wgb_ref, wub_ref):
    m = pl.program_id(1)

    @pl.when(_new_expert(be_ref, m))
    def _():
        wgb_ref[...] = wg_ref[...].astype(BF16)
        wub_ref[...] = wu_ref[...].astype(BF16)

    @pl.when(m < nu_ref[0])
    def _():
        xb = jnp.concatenate([_load_chunk(x_ref, j, MOE_TM).astype(BF16) for j in range(CHUNKS)], axis=1)
        g = jnp.minimum(_dot(xb, wgb_ref[...]) + bg_ref[...], SWIGLU_LIMIT)
        u = jnp.clip(_dot(xb, wub_ref[...]) + bu_ref[...], -SWIGLU_LIMIT, SWIGLU_LIMIT)
        o_ref[...] = (g * jax.nn.sigmoid(SWIGLU_ALPHA * g) * (u + 1.0)).astype(o_ref.dtype)

    @pl.when(m >= nu_ref[0])
    def _():
        o_ref[...] = jnp.zeros_like(o_ref)


def _gate_up(layer, xs, block_e, n_used, w_gate, b_gate, w_up, b_up):
    tm, tn = MOE_TM, MOE_TN
    wspec = pl.BlockSpec((None, None, D_MODEL, tn), lambda j, i, be, nu: (layer, be[i], 0, j))
    bspec = pl.BlockSpec((None, None, 1, tn), lambda j, i, be, nu: (layer, be[i], 0, j))
    return pl.pallas_call(
        _gate_up_kernel,
        grid_spec=pltpu.PrefetchScalarGridSpec(
            num_scalar_prefetch=2,
            grid=(D_FF // tn, MOE_BLOCKS),
            in_specs=[pl.BlockSpec((tm * CHUNKS, LANES), lambda j, i, be, nu: (i, 0)),
                      wspec, bspec, wspec, bspec],
            out_specs=pl.BlockSpec((tm, tn), lambda j, i, be, nu: (i, j)),
            scratch_shapes=[pltpu.VMEM((D_MODEL, tn), BF16), pltpu.VMEM((D_MODEL, tn), BF16)],
        ),
        out_shape=jax.ShapeDtypeStruct((MOE_ROWS, D_FF), BF16),
        compiler_params=_params(56, 2),
        name="moe_gate_up",
    )(block_e, n_used, xs, w_gate, b_gate.reshape(DEPTH, N_EXPERTS, 1, D_FF),
      w_up, b_up.reshape(DEPTH, N_EXPERTS, 1, D_FF))


def _down_kernel(be_ref, nu_ref, h_ref, wd_ref, bd_ref, o_ref, wdb_ref):
    m = pl.program_id(0)

    @pl.when(_new_expert(be_ref, m))
    def _():
        wdb_ref[...] = wd_ref[...].astype(BF16)

    @pl.when(m < nu_ref[0])
    def _():
        _store_chunked(o_ref, _dot(h_ref[...], wdb_ref[...]) + bd_ref[...], MOE_TM)

    @pl.when(m >= nu_ref[0])
    def _():
        o_ref[...] = jnp.zeros_like(o_ref)


def _down(layer, hs, block_e, n_used, w_down, b_down):
    tm = MOE_TM
    return pl.pallas_call(
        _down_kernel,
        grid_spec=pltpu.PrefetchScalarGridSpec(
            num_scalar_prefetch=2,
            grid=(MOE_BLOCKS,),
            in_specs=[pl.BlockSpec((tm, D_FF), lambda i, be, nu: (i, 0)),
                      pl.BlockSpec((None, None, D_FF, D_MODEL), lambda i, be, nu: (layer, be[i], 0, 0)),
                      pl.BlockSpec((None, None, 1, D_MODEL), lambda i, be, nu: (layer, be[i], 0, 0))],
            out_specs=pl.BlockSpec((tm * CHUNKS, LANES), lambda i, be, nu: (i, 0)),
            scratch_shapes=[pltpu.VMEM((D_FF, D_MODEL), BF16)],
        ),
        out_shape=jax.ShapeDtypeStruct((MOE_ROWS * CHUNKS, LANES), F32),
        compiler_params=_params(56, 1),
        name="moe_down",
    )(block_e, n_used, hs, w_down, b_down.reshape(DEPTH, N_EXPERTS, 1, D_MODEL))


COMBINE_T = 128


def _combine_kernel(dest_ref, next_ref, ys_hbm, gates_ref, x_ref, gate_ref, lng_ref, lnb_ref, xo_ref,
                    buf_ref, sem):
    i = pl.program_id(0)
    last = pl.num_programs(0) - 1
    slot = i % 2

    def copy(ref, s, t, k):
        src = pl.multiple_of(ref[0, 0, t * TOP_K + k] * CHUNKS, CHUNKS)
        return pltpu.make_async_copy(ys_hbm.at[pl.ds(src, CHUNKS), :],
                                     buf_ref.at[s, k, pl.ds(t * CHUNKS, CHUNKS), :], sem.at[s])

    def issue(ref, s):
        def start(t, c):
            for k in range(TOP_K):
                copy(ref, s, t, k).start()
            return c

        lax.fori_loop(0, COMBINE_T, start, 0, unroll=4)

    @pl.when(i == 0)
    def _():
        issue(dest_ref, 0)

    @pl.when(i < last)
    def _():
        issue(next_ref, 1 - slot)

    for t in range(COMBINE_T):
        for k in range(TOP_K):
            copy(dest_ref, slot, t, k).wait()
    gates = gates_ref[...]
    gk = [jnp.broadcast_to(gates[:, k:k + 1], (COMBINE_T, LANES)) for k in range(TOP_K)]
    parts = []
    for j in range(CHUNKS):
        rows = [gk[k] * _load_chunk(buf_ref.at[slot, k], j, COMBINE_T) for k in range(TOP_K)]
        parts.append((rows[0] + rows[1]) + (rows[2] + rows[3]))
    f = jnp.concatenate(parts, axis=1)
    z = DEEPNORM_ALPHA * x_ref[...] + gate_ref[0] * f
    xo_ref[...] = _layer_norm(z, lng_ref[...], lnb_ref[...])


def _combine(ys, dest, gates, x, gate, ln_g, ln_b):
    t = COMBINE_T
    steps = N_TOK // t
    d = D_MODEL
    dest = dest.reshape(steps, 1, t * TOP_K)
    return pl.pallas_call(
        _combine_kernel,
        grid=(steps,),
        in_specs=[
            pl.BlockSpec((1, 1, t * TOP_K), lambda i: (i, 0, 0), memory_space=pltpu.SMEM),
            pl.BlockSpec((1, 1, t * TOP_K), lambda i: (jnp.minimum(i + 1, steps - 1), 0, 0),
                         memory_space=pltpu.SMEM),
            pl.BlockSpec(memory_space=pl.ANY),
            pl.BlockSpec((t, TOP_K), lambda i: (i, 0)),
            pl.BlockSpec((t, d), lambda i: (i, 0)),
            pl.BlockSpec((1, 1, d), lambda i: (i * t // SEG, 0, 0)),
            pl.BlockSpec((1, d), lambda i: (0, 0)),
            pl.BlockSpec((1, d), lambda i: (0, 0)),
        ],
        out_specs=pl.BlockSpec((t, d), lambda i: (i, 0)),
        out_shape=jax.ShapeDtypeStruct((N_TOK, d), F32),
        scratch_shapes=[pltpu.VMEM((2, TOP_K, t * CHUNKS, LANES), F32), pltpu.SemaphoreType.DMA((2,))],
        compiler_params=_params(32, 1),
        name="moe_combine",
    )(dest, dest, ys, gates, x, gate, ln_g.reshape(1, d), ln_b.reshape(1, d))


def _moe(layer, h, logits, x, gate, ln_g, ln_b, w_gate, b_gate, w_up, b_up, w_down, b_down):
    dest, gates, row_tok, block_e, n_used = _route(logits)
    xs = _gather_rows(h, row_tok)
    hs = _gate_up(layer, xs, block_e, n_used, w_gate, b_gate, w_up, b_up)
    ys = _down(layer, hs, block_e, n_used, w_down, b_down)
    return _combine(ys, dest, gates, x, gate, ln_g, ln_b)


def _even_mixer(p, i, tabs, sink, qg, kg, cache_k_a, cache_v_a, cache_k_b, cache_v_b):
    pb, sb = N_PROMPT // SEQ, N_PROMPT // DEC_SEQ
    norm = (qg, kg)
    oa_p = _attention(p, nb=BATCH, T=SEQ, row0=0, q_col=0, k_col=1024, v_col=1280,
                      n_kv=KV_A, G=H_A // KV_A, tq=SEQ, sink=sink)
    ob_p, kb_n = _attention(p, nb=BATCH, T=SEQ, row0=0, q_col=1536, k_col=2560, v_col=2816,
                            n_kv=KV_B, G=H_B // KV_B, tq=SEQ, norm=norm, emit_kn=True)
    oa_s = _attention(p, nb=DEC_BATCH, T=DEC_SEQ, row0=sb, q_col=0, k_col=1024, v_col=1280,
                      n_kv=KV_A, G=H_A // KV_A, tq=WINDOW, tabs=tabs, ctx=(cache_k_a, cache_v_a),
                      layer=i, window=True, sink=sink)
    ob_s = _attention(p, nb=DEC_BATCH, T=DEC_SEQ, row0=sb, q_col=1536, k_col=2560, v_col=2816,
                      n_kv=KV_B, G=H_B // KV_B, tq=256, tabs=tabs, ctx=(cache_k_b, cache_v_b),
                      layer=i, norm=norm)
    del pb
    pp = p[:N_PROMPT]
    ka = pp[:, 1024:1280].reshape(BATCH, SEQ, KV_A, HEAD_DIM)
    va = pp[:, 1280:1536].reshape(BATCH, SEQ, KV_A, HEAD_DIM)
    kb = kb_n.reshape(BATCH, SEQ, KV_B, HEAD_DIM)
    vb = pp[:, 2816:3072].reshape(BATCH, SEQ, KV_B, HEAD_DIM)
    return (jnp.concatenate([oa_p, oa_s], 0), jnp.concatenate([ob_p, ob_s], 0), (ka, va, kb, vb))


def _odd_mixer(p, i, layer, tabs, decay, lam, subln, state_f, state_b, cache_k_d, cache_v_d):
    sb = N_PROMPT // DEC_SEQ
    lam_init = 0.8 - 0.6 * math.exp(-0.3 * layer)
    yc_p, sf, sbk = _retention(p, nb=BATCH, T=SEQ, row0=0, decay=decay)
    yc_s, _, _ = _retention(p, nb=DEC_BATCH, T=DEC_SEQ, row0=sb, decay=decay, tabs=tabs,
                            state=(state_f, state_b), layer=i)
    yd_p = _diff_attention(p, nb=BATCH, T=SEQ, row0=0, tq=SEQ, lam=lam, subln=subln, lam_init=lam_init)
    yd_s = _diff_attention(p, nb=DEC_BATCH, T=DEC_SEQ, row0=sb, tq=256, lam=lam, subln=subln,
                           lam_init=lam_init, tabs=tabs, ctx=(cache_k_d, cache_v_d), layer=i)
    pp = p[:N_PROMPT]
    kd = pp[:, 6144:7168].reshape(BATCH, SEQ, H_D, 2 * HEAD_DIM)
    vd = pp[:, 7168:8192].reshape(BATCH, SEQ, H_D, 2 * HEAD_DIM)
    return (jnp.concatenate([yc_p, yc_s], 0), jnp.concatenate([yd_p, yd_s], 0), (sf, sbk, kd, vd))


def kernel(x_prompt, x_sample, c, cache_k_a, cache_v_a, cache_k_b, cache_v_b, state_ret_fwd, state_ret_bwd, cache_k_d, cache_v_d, c_ctx, w_ada, b_ada, ln_g, ln_b, w_in_ab, w_out_ab, sink_a, qnorm_b, knorm_b, w_in_cd, w_out_cd, ret_decay, lam_d, subln_d, w_router, b_router, w_gate, b_gate, w_up, b_up, w_down, b_down):
    x = jnp.concatenate([x_prompt.reshape(N_PROMPT, D_MODEL), x_sample.reshape(N_SAMPLE, D_MODEL)], 0)
    cond = jnp.concatenate([c_ctx[None], c, jnp.zeros((16 - 1 - DEC_BATCH, D_MODEL), F32)], 0)
    mods = _ada(cond, w_ada, b_ada)
    seg_row = jnp.array([0] * (N_PROMPT // SEG) + list(range(1, DEC_BATCH + 1)), jnp.int32)
    mods = mods[:, seg_row].reshape(DEPTH, N_SEG, 6, 1, D_MODEL)
    tabs = _rope_tables()
    caches = ([], [], [], [], [], [], [], [])
    for l in range(DEPTH):
        i = l // 2
        mod = [mods[l, :, j] for j in range(6)]
        if l % 2 == 0:
            p = _proj(x, mod[1], mod[0], w_in_ab, i)
            a1, a2, new = _even_mixer(p, i, tabs, sink_a[i], qnorm_b[i], knorm_b[i],
                                      cache_k_a, cache_v_a, cache_k_b, cache_v_b)
            w_out = w_out_ab[i]
            for dst, val in zip(caches[:4], new):
                dst.append(val)
        else:
            p = _proj(x, mod[1], mod[0], w_in_cd, i)
            a1, a2, new = _odd_mixer(p, i, l, tabs, ret_decay[i], lam_d[i], subln_d[i],
                                     state_ret_fwd, state_ret_bwd, cache_k_d, cache_v_d)
            w_out = w_out_cd[i]
            for dst, val in zip(caches[4:], new):
                dst.append(val)
        x, h, logits = _oproj(a1, a2, w_out, x, mod[2], mod[4], mod[3], ln_g[l, 0], ln_b[l, 0],
                              w_router[l], b_router[l])
        x = _moe(l, h, logits, x, mod[5], ln_g[l, 1], ln_b[l, 1],
                 w_gate, b_gate, w_up, b_up, w_down, b_down)
    xp = x[:N_PROMPT].reshape(BATCH, SEQ, D_MODEL)
    xs = x[N_PROMPT:].reshape(DEC_BATCH, DEC_SEQ, D_MODEL)
    return (xp, xs) + tuple(jnp.stack(v, 1) for v in caches)
```

```python
import functools
import math

import jax
import jax.numpy as jnp
from jax import lax
from jax.experimental import pallas as pl
from jax.experimental.pallas import tpu as pltpu

F32 = jnp.float32
BF16 = jnp.bfloat16

D_MODEL = 2048
BATCH = 16
SEQ = 256
DEPTH = 4
DEC_BATCH = 8
DEC_SEQ = 1024
PAST_LEN = 256
GRID_W = 64
HEAD_DIM = 128
H_A = 8
KV_A = 2
WINDOW = 128
H_B = 8
KV_B = 2
H_C = 8
H_D = 4
RET_CHUNK = 128
ROPE_THETA = 10000.0
N_EXPERTS = 32
TOP_K = 4
D_FF = 2048
SWIGLU_LIMIT = 7.0
SWIGLU_ALPHA = 1.702
DEEPNORM_ALPHA = (2 * DEPTH) ** 0.25
NORM_EPS = 1e-6

N_PROMPT = BATCH * SEQ
N_SAMPLE = DEC_BATCH * DEC_SEQ
N_TOK = N_PROMPT + N_SAMPLE
SEG = DEC_SEQ
N_SEG = N_TOK // SEG
EVEN_IN = 3072
ODD_IN = 8192

MOE_TM = 256
MOE_TN = 1024
LANES = 128
CHUNKS = D_MODEL // LANES
N_ASSIGN = N_TOK * TOP_K
MOE_BLOCKS = (N_ASSIGN + N_EXPERTS * (MOE_TM - 1) + MOE_TM - 1) // MOE_TM
MOE_ROWS = MOE_BLOCKS * MOE_TM

MIB = 1024 * 1024


def _params(vmem_mib, n_axes):
    return pltpu.CompilerParams(
        dimension_semantics=("arbitrary",) * n_axes,
        vmem_limit_bytes=vmem_mib * MIB)


def _dot(a, b):
    return jnp.dot(a, b, preferred_element_type=F32)


def _dot_nt(a, b):
    return lax.dot_general(a, b, (((1,), (1,)), ((), ())), preferred_element_type=F32)


def _layer_norm(z, g, b):
    mu = jnp.mean(z, -1, keepdims=True)
    zc = z - mu
    var = jnp.mean(zc * zc, -1, keepdims=True)
    return zc * lax.rsqrt(var + NORM_EPS) * g + b


def _ada_kernel(c_ref, w_ref, b_ref, o_ref):
    c = c_ref[...]
    a = c * jax.nn.sigmoid(c)
    o_ref[0] = _dot(a.astype(BF16), w_ref[0].astype(BF16)) + b_ref[0]


def _ada(cond, w_ada, b_ada):
    tn = 1024
    n = w_ada.shape[-1]
    return pl.pallas_call(
        _ada_kernel,
        grid=(DEPTH, n // tn),
        in_specs=[
            pl.BlockSpec((16, D_MODEL), lambda l, j: (0, 0)),
            pl.BlockSpec((1, D_MODEL, tn), lambda l, j: (l, 0, j)),
            pl.BlockSpec((1, 1, tn), lambda l, j: (l, 0, j)),
        ],
        out_specs=pl.BlockSpec((1, 16, tn), lambda l, j: (l, 0, j)),
        out_shape=jax.ShapeDtypeStruct((DEPTH, 16, n), F32),
        compiler_params=_params(40, 2),
        name="ada",
    )(cond, w_ada, b_ada.reshape(DEPTH, 1, n))


def _proj_kernel(x_ref, sc_ref, sh_ref, w_ref, o_ref, wb_ref):
    @pl.when(pl.program_id(1) == 0)
    def _():
        wb_ref[...] = w_ref[...].astype(BF16)

    h = x_ref[...] * (1.0 + sc_ref[0]) + sh_ref[0]
    o_ref[...] = _dot(h.astype(BF16), wb_ref[...])


def _proj(x, scale, shift, w, layer):
    tm, tn = 512, 1024
    m, k = x.shape
    n = w.shape[2]
    return pl.pallas_call(
        _proj_kernel,
        grid=(n // tn, m // tm),
        in_specs=[
            pl.BlockSpec((tm, k), lambda j, i: (i, 0)),
            pl.BlockSpec((1, 1, k), lambda j, i: (i * tm // SEG, 0, 0)),
            pl.BlockSpec((1, 1, k), lambda j, i: (i * tm // SEG, 0, 0)),
            pl.BlockSpec((None, k, tn), lambda j, i: (layer, 0, j)),
        ],
        out_specs=pl.BlockSpec((tm, tn), lambda j, i: (i, j)),
        out_shape=jax.ShapeDtypeStruct((m, n), F32),
        scratch_shapes=[pltpu.VMEM((k, tn), BF16)],
        compiler_params=_params(48, 2),
        name="proj",
    )(x, scale, shift, w)


def _store_chunked(ref, val, n):
    for j in range(CHUNKS):
        ref[pl.ds(j, n, stride=CHUNKS), :] = val[:, j * LANES:(j + 1) * LANES]


def _load_chunk(ref, j, n):
    return ref[pl.ds(j, n, stride=CHUNKS), :]


def _oproj_kernel(a1_ref, a2_ref, w1_ref, w2_ref, x_ref, gate_ref, sc_ref, sh_ref,
                  lng_ref, lnb_ref, wr_ref, br_ref, xo_ref, h_ref, lg_ref):
    y = _dot(a1_ref[...], w1_ref[...]) + _dot(a2_ref[...], w2_ref[...])
    z = DEEPNORM_ALPHA * x_ref[...] + gate_ref[0] * y
    xn = _layer_norm(z, lng_ref[...], lnb_ref[...])
    xo_ref[...] = xn
    h = xn * (1.0 + sc_ref[0]) + sh_ref[0]
    _store_chunked(h_ref, h, h.shape[0])
    lg_ref[...] = _dot(h.astype(BF16), wr_ref[...]) + br_ref[...]


def _oproj(a1, a2, w_out, x, gate, scale, shift, ln_g, ln_b, w_router, b_router):
    tm = 256
    m, d = x.shape
    half = a1.shape[1]
    wb = w_out.astype(BF16)
    seg = lambda i: (i * tm // SEG, 0, 0)
    row = lambda i: (i, 0)
    fix = lambda i: (0, 0)
    return pl.pallas_call(
        _oproj_kernel,
        grid=(m // tm,),
        in_specs=[
            pl.BlockSpec((tm, half), row),
            pl.BlockSpec((tm, half), row),
            pl.BlockSpec((half, d), fix),
            pl.BlockSpec((half, d), lambda i: (1, 0)),
            pl.BlockSpec((tm, d), row),
            pl.BlockSpec((1, 1, d), seg),
            pl.BlockSpec((1, 1, d), seg),
            pl.BlockSpec((1, 1, d), seg),
            pl.BlockSpec((1, d), fix),
            pl.BlockSpec((1, d), fix),
            pl.BlockSpec((d, N_EXPERTS), fix),
            pl.BlockSpec((1, N_EXPERTS), fix),
        ],
        out_specs=[
            pl.BlockSpec((tm, d), row),
            pl.BlockSpec((tm * CHUNKS, LANES), row),
            pl.BlockSpec((tm, N_EXPERTS), row),
        ],
        out_shape=[
            jax.ShapeDtypeStruct((m, d), F32),
            jax.ShapeDtypeStruct((m * CHUNKS, LANES), F32),
            jax.ShapeDtypeStruct((m, N_EXPERTS), F32),
        ],
        compiler_params=_params(48, 1),
        name="oproj",
    )(a1, a2, wb, wb, x, gate, scale, shift, ln_g.reshape(1, d), ln_b.reshape(1, d),
      w_router.astype(BF16), b_router.reshape(1, N_EXPERTS))


def _rope_tables():
    t = jnp.arange(DEC_SEQ)
    rows = (t // GRID_W).astype(F32)
    cols = (t % GRID_W).astype(F32)
    n_freq = HEAD_DIM // 4
    inv = ROPE_THETA ** (-jnp.arange(n_freq, dtype=F32) / n_freq)
    ar = rows[:, None] * inv
    ac = cols[:, None] * inv
    zero = jnp.zeros_like(ar)
    c = jnp.concatenate([jnp.cos(ar), jnp.cos(ar), jnp.cos(ac), jnp.cos(ac)], -1)
    sa = jnp.concatenate([-jnp.sin(ar), zero, -jnp.sin(ac), zero], -1)
    sb = jnp.concatenate([zero, jnp.sin(ar), zero, jnp.sin(ac)], -1)
    return c, sa, sb


def _rope(x, c, sa, sb):
    return x * c + pltpu.roll(x, 96, 1) * sa + pltpu.roll(x, 32, 1) * sb


def _rms(x, g):
    return x * lax.rsqrt(jnp.mean(x * x, -1, keepdims=True) + NORM_EPS) * g


def _attn_kernel(*refs, T, G, tq, rope, ctx, window, sink, norm, emit_kn):
    it = iter(refs)
    q_ref, k_ref, v_ref = next(it), next(it), next(it)
    ck_ref = cv_ref = c_ref = sa_ref = sb_ref = sink_ref = qg_ref = kg_ref = kn_ref = None
    if ctx:
        ck_ref, cv_ref = next(it), next(it)
    if rope:
        c_ref, sa_ref, sb_ref = next(it), next(it), next(it)
    if sink:
        sink_ref = next(it)
    if norm:
        qg_ref, kg_ref = next(it), next(it)
    o_ref = next(it)
    if emit_kn:
        kn_ref = next(it)
    kr_ref, vr_ref = next(it), next(it)
    scale = HEAD_DIM ** -0.5
    kvh = pl.program_id(1)

    k = k_ref[...]
    if norm:
        k = _rms(k, kg_ref[...])
        if emit_kn:
            kn_ref[...] = k
    if rope:
        k = _rope(k, c_ref[...], sa_ref[...], sb_ref[...])
    kr_ref[...] = k.astype(BF16)
    vr_ref[...] = v_ref[...].astype(BF16)

    def block(i, carry):
        r0 = pl.multiple_of(i * tq, tq)
        if window:
            k0 = pl.multiple_of(jnp.clip(r0 - WINDOW, 0, T - 3 * WINDOW), WINDOW)
            nk = 3 * WINDOW
            kb = kr_ref[pl.ds(k0, nk), :]
            vb = vr_ref[pl.ds(k0, nk), :]
            qpos = r0 + lax.broadcasted_iota(jnp.int32, (tq, nk), 0)
            kpos = k0 + lax.broadcasted_iota(jnp.int32, (tq, nk), 1)
            keep = jnp.abs(qpos - kpos) <= WINDOW
        else:
            kb = kr_ref[...]
            vb = vr_ref[...]
        for h in range(G):
            q = q_ref[pl.ds(r0, tq), h * HEAD_DIM:(h + 1) * HEAD_DIM]
            if norm:
                q = _rms(q, qg_ref[...])
            if rope:
                q = _rope(q, c_ref[pl.ds(r0, tq), :], sa_ref[pl.ds(r0, tq), :], sb_ref[pl.ds(r0, tq), :])
            qb = q.astype(BF16)
            s = _dot_nt(qb, kb) * scale
            if window:
                s = jnp.where(keep, s, -1e30)
            m = jnp.max(s, -1, keepdims=True)
            if ctx:
                sc = _dot_nt(qb, ck_ref[...].astype(BF16)) * scale
                m = jnp.maximum(m, jnp.max(sc, -1, keepdims=True))
            if sink:
                sk = sink_ref[kvh * G + h]
                m = jnp.maximum(m, sk)
            p = jnp.exp(s - m)
            den = jnp.sum(p, -1, keepdims=True)
            o = _dot(p.astype(BF16), vb)
            if ctx:
                pc = jnp.exp(sc - m)
                den = den + jnp.sum(pc, -1, keepdims=True)
                o = o + _dot(pc.astype(BF16), cv_ref[...].astype(BF16))
            if sink:
                den = den + jnp.exp(sk - m)
            o_ref[pl.ds(r0, tq), h * HEAD_DIM:(h + 1) * HEAD_DIM] = (o / den).astype(o_ref.dtype)
        return carry

    lax.fori_loop(0, T // tq, block, 0)


def _attention(p, *, nb, T, row0, q_col, k_col, v_col, n_kv, G, tq, tabs=None, ctx=None,
               layer=0, window=False, sink=None, norm=None, emit_kn=False):
    gw = G * HEAD_DIM
    in_specs = [
        pl.BlockSpec((T, gw), lambda b, k: (row0 + b, q_col // gw + k)),
        pl.BlockSpec((T, HEAD_DIM), lambda b, k: (row0 + b, k_col // HEAD_DIM + k)),
        pl.BlockSpec((T, HEAD_DIM), lambda b, k: (row0 + b, v_col // HEAD_DIM + k)),
    ]
    args = [p, p, p]
    if ctx is not None:
        for a in ctx:
            args.append(a.reshape(a.shape[0], a.shape[1], PAST_LEN, n_kv * HEAD_DIM))
            in_specs.append(pl.BlockSpec((None, None, PAST_LEN, HEAD_DIM), lambda b, k: (b, layer, 0, k)))
    if tabs is not None:
        for a in tabs:
            args.append(a)
            in_specs.append(pl.BlockSpec((T, HEAD_DIM), lambda b, k: (0, 0)))
    if sink is not None:
        args.append(sink)
        in_specs.append(pl.BlockSpec(memory_space=pltpu.SMEM))
    if norm is not None:
        for a in norm:
            args.append(a.reshape(1, HEAD_DIM))
            in_specs.append(pl.BlockSpec((1, HEAD_DIM), lambda b, k: (0, 0)))
    out_specs = [pl.BlockSpec((T, gw), lambda b, k: (b, k))]
    out_shape = [jax.ShapeDtypeStruct((nb * T, n_kv * gw), BF16)]
    if emit_kn:
        out_specs.append(pl.BlockSpec((T, HEAD_DIM), lambda b, k: (b, k)))
        out_shape.append(jax.ShapeDtypeStruct((nb * T, n_kv * HEAD_DIM), F32))
    kern = functools.partial(
        _attn_kernel, T=T, G=G, tq=tq, rope=tabs is not None, ctx=ctx is not None,
        window=window, sink=sink is not None, norm=norm is not None, emit_kn=emit_kn)
    out = pl.pallas_call(
        kern,
        grid=(nb, n_kv),
        in_specs=in_specs,
        out_specs=out_specs,
        out_shape=out_shape,
        scratch_shapes=[pltpu.VMEM((T, HEAD_DIM), BF16), pltpu.VMEM((T, HEAD_DIM), BF16)],
        compiler_params=_params(40, 2),
        name="attn",
    )(*args)
    return out if emit_kn else out[0]


def _diff_kernel(*refs, T, tq, rope, ctx, lam_init):
    it = iter(refs)
    q_ref, k_ref, v_ref = next(it), next(it), next(it)
    ck_ref = cv_ref = c_ref = sa_ref = sb_ref = None
    if ctx:
        ck_ref, cv_ref = next(it), next(it)
    if rope:
        c_ref, sa_ref, sb_ref = next(it), next(it), next(it)
    lam_ref, sub_ref, o_ref, kr_ref, vr_ref = next(it), next(it), next(it), next(it), next(it)
    scale = HEAD_DIM ** -0.5
    D2 = 2 * HEAD_DIM

    for hh in range(2):
        k = k_ref[:, hh * HEAD_DIM:(hh + 1) * HEAD_DIM]
        if rope:
            k = _rope(k, c_ref[...], sa_ref[...], sb_ref[...])
        kr_ref[:, hh * HEAD_DIM:(hh + 1) * HEAD_DIM] = k.astype(BF16)
    vr_ref[...] = v_ref[...].astype(BF16)

    lam = lam_ref[...]
    lam_val = (jnp.exp(jnp.sum(lam[0:1] * lam[1:2], -1, keepdims=True))
               - jnp.exp(jnp.sum(lam[2:3] * lam[3:4], -1, keepdims=True)) + lam_init)

    def block(i, carry):
        r0 = pl.multiple_of(i * tq, tq)
        vb = vr_ref[...]
        outs = []
        for hh in range(2):
            cs = slice(hh * HEAD_DIM, (hh + 1) * HEAD_DIM)
            q = q_ref[pl.ds(r0, tq), cs]
            if rope:
                q = _rope(q, c_ref[pl.ds(r0, tq), :], sa_ref[pl.ds(r0, tq), :], sb_ref[pl.ds(r0, tq), :])
            qb = q.astype(BF16)
            s = _dot_nt(qb, kr_ref[:, cs]) * scale
            m = jnp.max(s, -1, keepdims=True)
            if ctx:
                sc = _dot_nt(qb, ck_ref[:, cs].astype(BF16)) * scale
                m = jnp.maximum(m, jnp.max(sc, -1, keepdims=True))
            p = jnp.exp(s - m)
            den = jnp.sum(p, -1, keepdims=True)
            o = _dot(p.astype(BF16), vb)
            if ctx:
                pc = jnp.exp(sc - m)
                den = den + jnp.sum(pc, -1, keepdims=True)
                o = o + _dot(pc.astype(BF16), cv_ref[...].astype(BF16))
            outs.append(o / den)
        d = outs[0] - lam_val * outs[1]
        o_ref[pl.ds(r0, tq), :] = (_rms(d, sub_ref[...]) * (1.0 - lam_init)).astype(o_ref.dtype)
        return carry

    lax.fori_loop(0, T // tq, block, 0)


def _diff_attention(p, *, nb, T, row0, tq, lam, subln, lam_init, tabs=None, ctx=None, layer=0):
    D2 = 2 * HEAD_DIM
    q_col, k_col, v_col = 5120, 6144, 7168
    in_specs = [
        pl.BlockSpec((T, D2), lambda b, h: (row0 + b, q_col // D2 + h)),
        pl.BlockSpec((T, D2), lambda b, h: (row0 + b, k_col // D2 + h)),
        pl.BlockSpec((T, D2), lambda b, h: (row0 + b, v_col // D2 + h)),
    ]
    args = [p, p, p]
    if ctx is not None:
        for a in ctx:
            args.append(a.reshape(a.shape[0], a.shape[1], PAST_LEN, H_D * D2))
            in_specs.append(pl.BlockSpec((None, None, PAST_LEN, D2), lambda b, h: (b, layer, 0, h)))
    if tabs is not None:
        for a in tabs:
            args.append(a)
            in_specs.append(pl.BlockSpec((T, HEAD_DIM), lambda b, h: (0, 0)))
    args += [lam, subln.reshape(1, D2)]
    in_specs += [pl.BlockSpec((4, HEAD_DIM), lambda b, h: (0, 0)),
                 pl.BlockSpec((1, D2), lambda b, h: (0, 0))]
    kern = functools.partial(_diff_kernel, T=T, tq=tq, rope=tabs is not None,
                             ctx=ctx is not None, lam_init=lam_init)
    return pl.pallas_call(
        kern,
        grid=(nb, H_D),
        in_specs=in_specs,
        out_specs=pl.BlockSpec((T, D2), lambda b, h: (b, h)),
        out_shape=jax.ShapeDtypeStruct((nb * T, H_D * D2), BF16),
        scratch_shapes=[pltpu.VMEM((T, D2), BF16), pltpu.VMEM((T, D2), BF16)],
        compiler_params=_params(40, 2),
        name="diff_attn",
    )(*args)


def _head_norm(x):
    mu = jnp.mean(x, -1, keepdims=True)
    xc = x - mu
    var = jnp.mean(xc * xc, -1, keepdims=True)
    return xc * lax.rsqrt(var + NORM_EPS)


def _silu(x):
    return x * jax.nn.sigmoid(x)


def _ret_kernel(*refs, T, rope, has_state):
    it = iter(refs)
    q_ref, k_ref, v_ref, gf_ref, gb_ref, dec_ref = (next(it) for _ in range(6))
    s0f_ref = s0b_ref = c_ref = sa_ref = sb_ref = None
    if has_state:
        s0f_ref, s0b_ref = next(it), next(it)
    if rope:
        c_ref, sa_ref, sb_ref = next(it), next(it), next(it)
    o_ref, sf_ref, sb_out_ref, acc_ref = next(it), next(it), next(it), next(it)
    C = RET_CHUNK
    n = T // C
    head = pl.program_id(1)
    ri = lax.broadcasted_iota(jnp.int32, (C, C), 0).astype(F32)
    ci = lax.broadcasted_iota(jnp.int32, (C, C), 1).astype(F32)

    def log_gamma(d):
        x = jnp.full((1, C), dec_ref[d, head], F32)
        return jnp.minimum(x, 0.0) - jnp.log1p(jnp.exp(-jnp.abs(x)))

    def chunk(c0):
        rows = pl.ds(c0, C)
        q = q_ref[rows, :]
        k = k_ref[rows, :] * (HEAD_DIM ** -0.5)
        if rope:
            tab = (c_ref[rows, :], sa_ref[rows, :], sb_ref[rows, :])
            q = _rope(q, *tab)
            k = _rope(k, *tab)
        return q, k, v_ref[rows, :].astype(BF16)

    def step(q, k, vb, s, dmat, xi, zeta, g_chunk):
        att = _dot_nt(q.astype(BF16), k.astype(BF16)) * dmat
        o = _dot(att.astype(BF16), vb) + _dot((q * xi).astype(BF16), s.astype(BF16))
        s = g_chunk * s + _dot((k * zeta).T.astype(BF16), vb)
        return o, s

    lg = log_gamma(0)
    diff = ri - ci
    dmat = jnp.where(diff >= 0, jnp.exp(jnp.maximum(diff, 0.0) * lg), 0.0)
    xi = jnp.exp((ri + 1.0) * lg)
    zeta = jnp.exp((C - 1.0 - ri) * lg)
    g_chunk = jnp.exp(C * lg)
    s = s0f_ref[...] if has_state else jnp.zeros((C, C), F32)
    for c in range(n):
        q, k, vb = chunk(c * C)
        o, s = step(q, k, vb, s, dmat, xi, zeta, g_chunk)
        acc_ref[pl.ds(c * C, C), :] = _silu(gf_ref[pl.ds(c * C, C), :]) * _head_norm(o)
    sf_ref[...] = s

    lg = log_gamma(1)
    diff = ci - ri
    dmat = jnp.where(diff >= 0, jnp.exp(jnp.maximum(diff, 0.0) * lg), 0.0)
    xi = jnp.exp((C - ri) * lg)
    zeta = jnp.exp(ri * lg)
    g_chunk = jnp.exp(C * lg)
    s = s0b_ref[...] if has_state else jnp.zeros((C, C), F32)
    for c in reversed(range(n)):
        q, k, vb = chunk(c * C)
        o, s = step(q, k, vb, s, dmat, xi, zeta, g_chunk)
        y = acc_ref[pl.ds(c * C, C), :] + _silu(gb_ref[pl.ds(c * C, C), :]) * _head_norm(o)
        o_ref[pl.ds(c * C, C), :] = y.astype(o_ref.dtype)
    sb_out_ref[...] = s


def _retention(p, *, nb, T, row0, decay, tabs=None, state=None, layer=0):
    hd = HEAD_DIM
    col = lambda c0: (lambda b, h: (row0 + b, c0 // hd + h))
    in_specs = [pl.BlockSpec((T, hd), col(c0)) for c0 in (0, 1024, 2048, 3072, 4096)]
    args = [p] * 5 + [decay]
    in_specs.append(pl.BlockSpec(memory_space=pltpu.SMEM))
    if state is not None:
        for a in state:
            args.append(a)
            in_specs.append(pl.BlockSpec((None, None, None, hd, hd), lambda b, h: (b, layer, h, 0, 0)))
    if tabs is not None:
        for a in tabs:
            args.append(a)
            in_specs.append(pl.BlockSpec((T, hd), lambda b, h: (0, 0)))
    kern = functools.partial(_ret_kernel, T=T, rope=tabs is not None, has_state=state is not None)
    st_spec = pl.BlockSpec((None, None, hd, hd), lambda b, h: (b, h, 0, 0))
    return pl.pallas_call(
        kern,
        grid=(nb, H_C),
        in_specs=in_specs,
        out_specs=[pl.BlockSpec((T, hd), lambda b, h: (b, h)), st_spec, st_spec],
        out_shape=[jax.ShapeDtypeStruct((nb * T, H_C * hd), BF16),
                   jax.ShapeDtypeStruct((nb, H_C, hd, hd), F32),
                   jax.ShapeDtypeStruct((nb, H_C, hd, hd), F32)],
        scratch_shapes=[pltpu.VMEM((T, hd), F32)],
        compiler_params=_params(40, 2),
        name="retention",
    )(*args)


def _route(logits):
    top_v, top_e = lax.top_k(logits, TOP_K)
    gates = jax.nn.softmax(top_v, axis=-1)
    chosen = jnp.sum((top_e[:, :, None] == jnp.arange(N_EXPERTS)[None, None, :]).astype(jnp.int32), 1)
    rank_all = jnp.cumsum(chosen, 0) - chosen
    counts = jnp.sum(chosen, 0)
    padded = (counts + MOE_TM - 1) // MOE_TM * MOE_TM
    ends_p = jnp.cumsum(padded)
    start_p = ends_p - padded
    dest = start_p[top_e] + jnp.take_along_axis(rank_all, top_e, 1)
    tok = jnp.repeat(jnp.arange(N_TOK, dtype=jnp.int32), TOP_K)
    row_tok = jnp.zeros((MOE_ROWS,), jnp.int32).at[dest.reshape(-1)].set(tok)
    block_start = jnp.arange(MOE_BLOCKS, dtype=jnp.int32) * MOE_TM
    block_e = jnp.minimum(jnp.sum((ends_p[None, :] <= block_start[:, None]).astype(jnp.int32), 1),
                          N_EXPERTS - 1).astype(jnp.int32)
    n_used = (ends_p[-1] // MOE_TM).astype(jnp.int32).reshape(1)
    return dest.astype(jnp.int32), gates, row_tok, block_e, n_used


GATHER_R = 256


def _gather_kernel(tok_ref, next_ref, h_hbm, o_ref, buf_ref, sem):
    i = pl.program_id(0)
    last = pl.num_programs(0) - 1
    slot = i % 2

    def copy(ref, s, r):
        src = pl.multiple_of(ref[0, 0, r] * CHUNKS, CHUNKS)
        return pltpu.make_async_copy(h_hbm.at[pl.ds(src, CHUNKS), :],
                                     buf_ref.at[s, pl.ds(r * CHUNKS, CHUNKS), :], sem.at[s])

    def issue(ref, s):
        def start(r, c):
            copy(ref, s, r).start()
            return c

        lax.fori_loop(0, GATHER_R, start, 0, unroll=8)

    @pl.when(i == 0)
    def _():
        issue(tok_ref, 0)

    @pl.when(i < last)
    def _():
        issue(next_ref, 1 - slot)

    for r in range(GATHER_R):
        copy(tok_ref, slot, r).wait()
    o_ref[...] = buf_ref[slot]


def _gather_rows(h, row_tok):
    steps = MOE_ROWS // GATHER_R
    tok = row_tok.reshape(steps, 1, GATHER_R)
    return pl.pallas_call(
        _gather_kernel,
        grid=(steps,),
        in_specs=[
            pl.BlockSpec((1, 1, GATHER_R), lambda i: (i, 0, 0), memory_space=pltpu.SMEM),
            pl.BlockSpec((1, 1, GATHER_R), lambda i: (jnp.minimum(i + 1, steps - 1), 0, 0),
                         memory_space=pltpu.SMEM),
            pl.BlockSpec(memory_space=pl.ANY),
        ],
        out_specs=pl.BlockSpec((GATHER_R * CHUNKS, LANES), lambda i: (i, 0)),
        out_shape=jax.ShapeDtypeStruct((MOE_ROWS * CHUNKS, LANES), F32),
        scratch_shapes=[pltpu.VMEM((2, GATHER_R * CHUNKS, LANES), F32), pltpu.SemaphoreType.DMA((2,))],
        compiler_params=_params(32, 1),
        name="moe_gather",
    )(tok, tok, h)


def _new_expert(be_ref, m):
    prev = be_ref[jnp.maximum(m - 1, 0)]
    return jnp.logical_or(m == 0, be_ref[m] != prev)


def _gate_up_kernel(be_ref, nu_ref, x_ref, wg_ref, bg_ref, wu_ref, bu_ref, o_ref, wgb_ref, wub_ref):
    m = pl.program_id(1)

    @pl.when(_new_expert(be_ref, m))
    def _():
        wgb_ref[...] = wg_ref[...].astype(BF16)
        wub_ref[...] = wu_ref[...].astype(BF16)

    @pl.when(m < nu_ref[0])
    def _():
        xb = jnp.concatenate([_load_chunk(x_ref, j, MOE_TM).astype(BF16) for j in range(CHUNKS)], axis=1)
        g = jnp.minimum(_dot(xb, wgb_ref[...]) + bg_ref[...], SWIGLU_LIMIT)
        u = jnp.clip(_dot(xb, wub_ref[...]) + bu_ref[...], -SWIGLU_LIMIT, SWIGLU_LIMIT)
        o_ref[...] = (g * jax.nn.sigmoid(SWIGLU_ALPHA * g) * (u + 1.0)).astype(o_ref.dtype)

    @pl.when(m >= nu_ref[0])
    def _():
        o_ref[...] = jnp.zeros_like(o_ref)


def _gate_up(layer, xs, block_e, n_used, w_gate, b_gate, w_up, b_up):
    tm, tn = MOE_TM, MOE_TN
    wspec = pl.BlockSpec((None, None, D_MODEL, tn), lambda j, i, be, nu: (layer, be[i], 0, j))
    bspec = pl.BlockSpec((None, None, 1, tn), lambda j, i, be, nu: (layer, be[i], 0, j))
    return pl.pallas_call(
        _gate_up_kernel,
        grid_spec=pltpu.PrefetchScalarGridSpec(
            num_scalar_prefetch=2,
            grid=(D_FF // tn, MOE_BLOCKS),
            in_specs=[pl.BlockSpec((tm * CHUNKS, LANES), lambda j, i, be, nu: (i, 0)),
                      wspec, bspec, wspec, bspec],
            out_specs=pl.BlockSpec((tm, tn), lambda j, i, be, nu: (i, j)),
            scratch_shapes=[pltpu.VMEM((D_MODEL, tn), BF16), pltpu.VMEM((D_MODEL, tn), BF16)],
        ),
        out_shape=jax.ShapeDtypeStruct((MOE_ROWS, D_FF), BF16),
        compiler_params=_params(56, 2),
        name="moe_gate_up",
    )(block_e, n_used, xs, w_gate, b_gate.reshape(DEPTH, N_EXPERTS, 1, D_FF),
      w_up, b_up.reshape(DEPTH, N_EXPERTS, 1, D_FF))


def _down_kernel(be_ref, nu_ref, h_ref, wd_ref, bd_ref, o_ref, wdb_ref):
    m = pl.program_id(0)

    @pl.when(_new_expert(be_ref, m))
    def _():
        wdb_ref[...] = wd_ref[...].astype(BF16)

    @pl.when(m < nu_ref[0])
    def _():
        _store_chunked(o_ref, _dot(h_ref[...], wdb_ref[...]) + bd_ref[...], MOE_TM)

    @pl.when(m >= nu_ref[0])
    def _():
        o_ref[...] = jnp.zeros_like(o_ref)


def _down(layer, hs, block_e, n_used, w_down, b_down):
    tm = MOE_TM
    return pl.pallas_call(
        _down_kernel,
        grid_spec=pltpu.PrefetchScalarGridSpec(
            num_scalar_prefetch=2,
            grid=(MOE_BLOCKS,),
            in_specs=[pl.BlockSpec((tm, D_FF), lambda i, be, nu: (i, 0)),
                      pl.BlockSpec((None, None, D_FF, D_MODEL), lambda i, be, nu: (layer, be[i], 0, 0)),
                      pl.BlockSpec((None, None, 1, D_MODEL), lambda i, be, nu: (layer, be[i], 0, 0))],
            out_specs=pl.BlockSpec((tm * CHUNKS, LANES), lambda i, be, nu: (i, 0)),
            scratch_shapes=[pltpu.VMEM((D_FF, D_MODEL), BF16)],
        ),
        out_shape=jax.ShapeDtypeStruct((MOE_ROWS * CHUNKS, LANES), F32),
        compiler_params=_params(56, 1),
        name="moe_down",
    )(block_e, n_used, hs, w_down, b_down.reshape(DEPTH, N_EXPERTS, 1, D_MODEL))


COMBINE_T = 128


def _combine_kernel(dest_ref, next_ref, ys_hbm, gates_ref, x_ref, gate_ref, lng_ref, lnb_ref, xo_ref,
                    buf_ref, sem):
    i = pl.program_id(0)
    last = pl.num_programs(0) - 1
    slot = i % 2

    def copy(ref, s, t, k):
        src = pl.multiple_of(ref[0, 0, t * TOP_K + k] * CHUNKS, CHUNKS)
        return pltpu.make_async_copy(ys_hbm.at[pl.ds(src, CHUNKS), :],
                                     buf_ref.at[s, k, pl.ds(t * CHUNKS, CHUNKS), :], sem.at[s])

    def issue(ref, s):
        def start(t, c):
            for k in range(TOP_K):
                copy(ref, s, t, k).start()
            return c

        lax.fori_loop(0, COMBINE_T, start, 0, unroll=4)

    @pl.when(i == 0)
    def _():
        issue(dest_ref, 0)

    @pl.when(i < last)
    def _():
        issue(next_ref, 1 - slot)

    for t in range(COMBINE_T):
        for k in range(TOP_K):
            copy(dest_ref, slot, t, k).wait()
    gates = gates_ref[...]
    gk = [jnp.broadcast_to(gates[:, k:k + 1], (COMBINE_T, LANES)) for k in range(TOP_K)]
    parts = []
    for j in range(CHUNKS):
        rows = [gk[k] * _load_chunk(buf_ref.at[slot, k], j, COMBINE_T) for k in range(TOP_K)]
        parts.append((rows[0] + rows[1]) + (rows[2] + rows[3]))
    f = jnp.concatenate(parts, axis=1)
    z = DEEPNORM_ALPHA * x_ref[...] + gate_ref[0] * f
    xo_ref[...] = _layer_norm(z, lng_ref[...], lnb_ref[...])


def _combine(ys, dest, gates, x, gate, ln_g, ln_b):
    t = COMBINE_T
    steps = N_TOK // t
    d = D_MODEL
    dest = dest.reshape(steps, 1, t * TOP_K)
    return pl.pallas_call(
        _combine_kernel,
        grid=(steps,),
        in_specs=[
            pl.BlockSpec((1, 1, t * TOP_K), lambda i: (i, 0, 0), memory_space=pltpu.SMEM),
            pl.BlockSpec((1, 1, t * TOP_K), lambda i: (jnp.minimum(i + 1, steps - 1), 0, 0),
                         memory_space=pltpu.SMEM),
            pl.BlockSpec(memory_space=pl.ANY),
            pl.BlockSpec((t, TOP_K), lambda i: (i, 0)),
            pl.BlockSpec((t, d), lambda i: (i, 0)),
            pl.BlockSpec((1, 1, d), lambda i: (i * t // SEG, 0, 0)),
            pl.BlockSpec((1, d), lambda i: (0, 0)),
            pl.BlockSpec((1, d), lambda i: (0, 0)),
        ],
        out_specs=pl.BlockSpec((t, d), lambda i: (i, 0)),
        out_shape=jax.ShapeDtypeStruct((N_TOK, d), F32),
        scratch_shapes=[pltpu.VMEM((2, TOP_K, t * CHUNKS, LANES), F32), pltpu.SemaphoreType.DMA((2,))],
        compiler_params=_params(32, 1),
        name="moe_combine",
    )(dest, dest, ys, gates, x, gate, ln_g.reshape(1, d), ln_b.reshape(1, d))


def _moe(layer, h, logits, x, gate, ln_g, ln_b, w_gate, b_gate, w_up, b_up, w_down, b_down):
    dest, gates, row_tok, block_e, n_used = _route(logits)
    xs = _gather_rows(h, row_tok)
    hs = _gate_up(layer, xs, block_e, n_used, w_gate, b_gate, w_up, b_up)
    ys = _down(layer, hs, block_e, n_used, w_down, b_down)
    return _combine(ys, dest, gates, x, gate, ln_g, ln_b)


def _even_mixer(p, i, tabs, sink, qg, kg, cache_k_a, cache_v_a, cache_k_b, cache_v_b):
    pb, sb = N_PROMPT // SEQ, N_PROMPT // DEC_SEQ
    norm = (qg, kg)
    oa_p = _attention(p, nb=BATCH, T=SEQ, row0=0, q_col=0, k_col=1024, v_col=1280,
                      n_kv=KV_A, G=H_A // KV_A, tq=SEQ, sink=sink)
    ob_p, kb_n = _attention(p, nb=BATCH, T=SEQ, row0=0, q_col=1536, k_col=2560, v_col=2816,
                            n_kv=KV_B, G=H_B // KV_B, tq=SEQ, norm=norm, emit_kn=True)
    oa_s = _attention(p, nb=DEC_BATCH, T=DEC_SEQ, row0=sb, q_col=0, k_col=1024, v_col=1280,
                      n_kv=KV_A, G=H_A // KV_A, tq=WINDOW, tabs=tabs, ctx=(cache_k_a, cache_v_a),
                      layer=i, window=True, sink=sink)
    ob_s = _attention(p, nb=DEC_BATCH, T=DEC_SEQ, row0=sb, q_col=1536, k_col=2560, v_col=2816,
                      n_kv=KV_B, G=H_B // KV_B, tq=256, tabs=tabs, ctx=(cache_k_b, cache_v_b),
                      layer=i, norm=norm)
    del pb
    pp = p[:N_PROMPT]
    ka = pp[:, 1024:1280].reshape(BATCH, SEQ, KV_A, HEAD_DIM)
    va = pp[:, 1280:1536].reshape(BATCH, SEQ, KV_A, HEAD_DIM)
    kb = kb_n.reshape(BATCH, SEQ, KV_B, HEAD_DIM)
    vb = pp[:, 2816:3072].reshape(BATCH, SEQ, KV_B, HEAD_DIM)
    return (jnp.concatenate([oa_p, oa_s], 0), jnp.concatenate([ob_p, ob_s], 0), (ka, va, kb, vb))


def _odd_mixer(p, i, layer, tabs, decay, lam, subln, state_f, state_b, cache_k_d, cache_v_d):
    sb = N_PROMPT // DEC_SEQ
    lam_init = 0.8 - 0.6 * math.exp(-0.3 * layer)
    yc_p, sf, sbk = _retention(p, nb=BATCH, T=SEQ, row0=0, decay=decay)
    yc_s, _, _ = _retention(p, nb=DEC_BATCH, T=DEC_SEQ, row0=sb, decay=decay, tabs=tabs,
                            state=(state_f, state_b), layer=i)
    yd_p = _diff_attention(p, nb=BATCH, T=SEQ, row0=0, tq=SEQ, lam=lam, subln=subln, lam_init=lam_init)
    yd_s = _diff_attention(p, nb=DEC_BATCH, T=DEC_SEQ, row0=sb, tq=256, lam=lam, subln=subln,
                           lam_init=lam_init, tabs=tabs, ctx=(cache_k_d, cache_v_d), layer=i)
    pp = p[:N_PROMPT]
    kd = pp[:, 6144:7168].reshape(BATCH, SEQ, H_D, 2 * HEAD_DIM)
    vd = pp[:, 7168:8192].reshape(BATCH, SEQ, H_D, 2 * HEAD_DIM)
    return (jnp.concatenate([yc_p, yc_s], 0), jnp.concatenate([yd_p, yd_s], 0), (sf, sbk, kd, vd))


def kernel(x_prompt, x_sample, c, cache_k_a, cache_v_a, cache_k_b, cache_v_b, state_ret_fwd, state_ret_bwd, cache_k_d, cache_v_d, c_ctx, w_ada, b_ada, ln_g, ln_b, w_in_ab, w_out_ab, sink_a, qnorm_b, knorm_b, w_in_cd, w_out_cd, ret_decay, lam_d, subln_d, w_router, b_router, w_gate, b_gate, w_up, b_up, w_down, b_down):
    x = jnp.concatenate([x_prompt.reshape(N_PROMPT, D_MODEL), x_sample.reshape(N_SAMPLE, D_MODEL)], 0)
    cond = jnp.concatenate([c_ctx[None], c, jnp.zeros((16 - 1 - DEC_BATCH, D_MODEL), F32)], 0)
    mods = _ada(cond, w_ada, b_ada)
    seg_row = jnp.array([0] * (N_PROMPT // SEG) + list(range(1, DEC_BATCH + 1)), jnp.int32)
    mods = mods[:, seg_row].reshape(DEPTH, N_SEG, 6, 1, D_MODEL)
    tabs = _rope_tables()
    caches = ([], [], [], [], [], [], [], [])
    for l in range(DEPTH):
        i = l // 2
        mod = [mods[l, :, j] for j in range(6)]
        if l % 2 == 0:
            p = _proj(x, mod[1], mod[0], w_in_ab, i)
            a1, a2, new = _even_mixer(p, i, tabs, sink_a[i], qnorm_b[i], knorm_b[i],
                                      cache_k_a, cache_v_a, cache_k_b, cache_v_b)
            w_out = w_out_ab[i]
            for dst, val in zip(caches[:4], new):
                dst.append(val)
        else:
            p = _proj(x, mod[1], mod[0], w_in_cd, i)
            a1, a2, new = _odd_mixer(p, i, l, tabs, ret_decay[i], lam_d[i], subln_d[i],
                                     state_ret_fwd, state_ret_bwd, cache_k_d, cache_v_d)
            w_out = w_out_cd[i]
            for dst, val in zip(caches[4:], new):
                dst.append(val)
        x, h, logits = _oproj(a1, a2, w_out, x, mod[2], mod[4], mod[3], ln_g[l, 0], ln_b[l, 0],
                              w_router[l], b_router[l])
        x = _moe(l, h, logits, x, mod[5], ln_g[l, 1], ln_b[l, 1],
                 w_gate, b_gate, w_up, b_up, w_down, b_down)
    xp = x[:N_PROMPT].reshape(BATCH, SEQ, D_MODEL)
    xs = x[N_PROMPT:].reshape(DEC_BATCH, DEC_SEQ, D_MODEL)
    return (xp, xs) + tuple(jnp.stack(v, 1) for v in caches)
```
